```python
import math
import jax, jax.numpy as jnp
from jax import lax
import numpy as np

D_MODEL = 1024
BATCH = 2
SEQ = 8192
DEPTH = 2
DEC_BATCH = 128
DEC_SEQ = 1
PAST_LEN = 2048
PAGE_SIZE = 128

N_A_LAYERS = DEPTH // 2
N_B_LAYERS = DEPTH - N_A_LAYERS
GROUP_SIZE = 16
N_GROUPS = D_MODEL // GROUP_SIZE
STATE_DIM = 64
N_HEADS = 8
HEAD_DIM = D_MODEL // N_HEADS // 2
ROT_DIM = HEAD_DIM // 4
ROPE_THETA = 500000.0
D_FF = ((8 * D_MODEL // 3 + 255) // 256) * 256
Q_BLOCK = 128
EPS = 1e-5
NEG = -1e30

kernel_name = "yoco_s5_diffattn_step"


def rmsnorm(x, g):
    xf = x.astype(jnp.float32)
    r = lax.rsqrt(jnp.mean(xf * xf, axis=-1, keepdims=True) + EPS)
    return (xf * r * g.astype(jnp.float32)).astype(x.dtype)


def rope_partial(x, pos):
    inv = ROPE_THETA ** (-jnp.arange(0, ROT_DIM, 2, dtype=jnp.float32) / ROT_DIM)
    ang = pos.astype(jnp.float32)[:, None] * inv[None, :]
    cos = jnp.cos(ang)[None, :, None, None, :]
    sin = jnp.sin(ang)[None, :, None, None, :]
    xf = x.astype(jnp.float32)
    half = ROT_DIM // 2
    x1, x2, xp = xf[..., :half], xf[..., half:ROT_DIM], xf[..., ROT_DIM:]
    out = jnp.concatenate([x1 * cos - x2 * sin, x2 * cos + x1 * sin, xp], axis=-1)
    return out.astype(x.dtype)


def swiglu(h, wg, wu, wd):
    return (jax.nn.silu(h @ wg) * (h @ wu)) @ wd


def s5_mixer(u, h0_re, h0_im, a_re, a_im, log_dt, b_re, b_im, c_re, c_im, d, w_ga, w_gb):
    bsz, seqlen, _ = u.shape
    f32 = jnp.float32
    a_re = a_re.astype(f32); a_im = a_im.astype(f32)
    dt = jnp.exp(log_dt.astype(f32))[:, None]
    mag = jnp.exp(dt * a_re)
    ang = dt * a_im
    abar_re = mag * jnp.cos(ang)
    abar_im = mag * jnp.sin(ang)
    den = a_re * a_re + a_im * a_im
    nr = abar_re - 1.0
    ni = abar_im
    f_re = (nr * a_re + ni * a_im) / den
    f_im = (ni * a_re - nr * a_im) / den
    b_re = b_re.astype(f32); b_im = b_im.astype(f32)
    bbar_re = f_re[..., None] * b_re - f_im[..., None] * b_im
    bbar_im = f_re[..., None] * b_im + f_im[..., None] * b_re
    ug = u.astype(f32).reshape(bsz, seqlen, N_GROUPS, GROUP_SIZE)
    bu_re = jnp.einsum("blgh,gph->blgp", ug, bbar_re)
    bu_im = jnp.einsum("blgh,gph->blgp", ug, bbar_im)
    ar = jnp.broadcast_to(abar_re, bu_re.shape)
    ai = jnp.broadcast_to(abar_im, bu_re.shape)

    def combine(e1, e2):
        ar1, ai1, br1, bi1 = e1
        ar2, ai2, br2, bi2 = e2
        return (ar2 * ar1 - ai2 * ai1,
                ar2 * ai1 + ai2 * ar1,
                ar2 * br1 - ai2 * bi1 + br2,
                ar2 * bi1 + ai2 * br1 + bi2)

    acum_re, acum_im, hs_re, hs_im = lax.associative_scan(combine, (ar, ai, bu_re, bu_im), axis=1)
    h0r = h0_re.astype(f32)[:, None]
    h0i = h0_im.astype(f32)[:, None]
    h_re = hs_re + acum_re * h0r - acum_im * h0i
    h_im = hs_im + acum_re * h0i + acum_im * h0r
    y = (jnp.einsum("blgp,ghp->blgh", h_re, c_re.astype(f32))
         - jnp.einsum("blgp,ghp->blgh", h_im, c_im.astype(f32)))
    y = y.reshape(bsz, seqlen, D_MODEL) + d.astype(f32) * u.astype(f32)
    z = jax.nn.gelu(y).astype(u.dtype)
    out = (z @ w_ga) * jax.nn.sigmoid(z @ w_gb)
    return out, h_re[:, -1].astype(h0_re.dtype), h_im[:, -1].astype(h0_im.dtype)


def diff_attention(q, k, v, q_pos, k_pos, lam):
    s = jnp.einsum("bqhcd,bkhcd->bhcqk", q, k, preferred_element_type=jnp.float32) * (HEAD_DIM ** -0.5)
    mask = k_pos[None, :] <= q_pos[:, None]
    s = jnp.where(mask, s, NEG)
    p = jax.nn.softmax(s, axis=-1)
    w = p[:, :, 0] - lam * p[:, :, 1]
    return jnp.einsum("bhqk,bkhe->bqhe", w.astype(v.dtype), v)


def trunk(x, pos, h0_re, h0_im, attend, p):
    bsz, seqlen, _ = x.shape
    new_re, new_im = [], []
    k = v = None
    for layer in range(DEPTH):
        h = rmsnorm(x, p["mix_norm"][layer])
        if layer < N_A_LAYERS:
            i = layer
            y, hr, hi = s5_mixer(h, h0_re[i], h0_im[i], p["ssm_a_re"][i], p["ssm_a_im"][i],
                                 p["ssm_log_dt"][i], p["ssm_b_re"][i], p["ssm_b_im"][i],
                                 p["ssm_c_re"][i], p["ssm_c_im"][i], p["ssm_d"][i],
                                 p["glu_w_a"][i], p["glu_w_b"][i])
            new_re.append(hr)
            new_im.append(hi)
        else:
            j = layer - N_A_LAYERS
            if k is None:
                kv_in = rmsnorm(x, p["kv_norm"])
                k = rope_partial((kv_in @ p["w_k"]).reshape(bsz, seqlen, N_HEADS, 2, HEAD_DIM), pos)
                v = (kv_in @ p["w_v"]).reshape(bsz, seqlen, N_HEADS, 2 * HEAD_DIM)
            q = rope_partial((h @ p["w_q"][j]).reshape(bsz, seqlen, N_HEADS, 2, HEAD_DIM), pos)
            lam_init = 0.8 - 0.6 * math.exp(-0.3 * layer)
            lam = (jnp.exp(jnp.sum(p["lambda_q1"][j].astype(jnp.float32) * p["lambda_k1"][j].astype(jnp.float32)))
                   - jnp.exp(jnp.sum(p["lambda_q2"][j].astype(jnp.float32) * p["lambda_k2"][j].astype(jnp.float32)))
                   + lam_init)
            o = attend(q, k, v, pos, lam)
            o = rmsnorm(o, p["subln"][j]) * (1.0 - lam_init)
            y = o.reshape(bsz, seqlen, D_MODEL) @ p["w_o"][j]
        x = x + y
        x = x + swiglu(rmsnorm(x, p["ffn_norm"][layer]), p["ffn_w_gate"][layer],
                       p["ffn_w_up"][layer], p["ffn_w_down"][layer])
    return rmsnorm(x, p["final_norm"]), k, v, jnp.stack(new_re), jnp.stack(new_im)


def setup_inputs(seed: int = 0) -> dict:
    key = jax.random.key(seed)
    ks = jax.random.split(key, 40)
    nrm = jax.random.normal
    n_pages = PAST_LEN // PAGE_SIZE
    n_used = DEC_BATCH * n_pages
    n_phys = n_used + max(1, n_used // 4)
    hd2 = N_HEADS * 2 * HEAD_DIM
    inp = {}
    inp["x_prompt"] = nrm(ks[0], (BATCH, SEQ, D_MODEL), jnp.float32)
    inp["x_sample"] = nrm(ks[1], (DEC_BATCH, DEC_SEQ, D_MODEL), jnp.float32)
    inp["cache_k"] = nrm(ks[2], (n_phys, PAGE_SIZE, N_HEADS, 2, HEAD_DIM), jnp.float32)
    inp["cache_v"] = nrm(ks[3], (n_phys, PAGE_SIZE, N_HEADS, 2 * HEAD_DIM), jnp.float32)
    inp["state_ssm_re"] = 0.1 * nrm(ks[4], (N_A_LAYERS, DEC_BATCH, N_GROUPS, STATE_DIM), jnp.float32)
    inp["state_ssm_im"] = 0.1 * nrm(ks[5], (N_A_LAYERS, DEC_BATCH, N_GROUPS, STATE_DIM), jnp.float32)
    inp["page_table"] = jax.random.permutation(ks[6], n_phys)[:n_used].reshape(DEC_BATCH, n_pages).astype(jnp.int32)
    inp["mix_norm"] = 1.0 + 0.01 * nrm(ks[7], (DEPTH, D_MODEL), jnp.float32)
    inp["ffn_norm"] = 1.0 + 0.01 * nrm(ks[8], (DEPTH, D_MODEL), jnp.float32)
    inp["kv_norm"] = 1.0 + 0.01 * nrm(ks[9], (D_MODEL,), jnp.float32)
    inp["final_norm"] = 1.0 + 0.01 * nrm(ks[10], (D_MODEL,), jnp.float32)
    inp["ssm_a_re"] = -0.5 + 0.01 * nrm(ks[11], (N_A_LAYERS, N_GROUPS, STATE_DIM), jnp.float32)
    inp["ssm_a_im"] = (math.pi * jnp.arange(STATE_DIM, dtype=jnp.float32)[None, None, :]
                       + 0.01 * nrm(ks[12], (N_A_LAYERS, N_GROUPS, STATE_DIM), jnp.float32))
    inp["ssm_log_dt"] = jax.random.uniform(ks[13], (N_A_LAYERS, N_GROUPS), jnp.float32,
                                           math.log(0.001), math.log(0.1))
    inp["ssm_b_re"] = nrm(ks[14], (N_A_LAYERS, N_GROUPS, STATE_DIM, GROUP_SIZE), jnp.float32) * (2 * GROUP_SIZE) ** -0.5
    inp["ssm_b_im"] = nrm(ks[15], (N_A_LAYERS, N_GROUPS, STATE_DIM, GROUP_SIZE), jnp.float32) * (2 * GROUP_SIZE) ** -0.5
    inp["ssm_c_re"] = nrm(ks[16], (N_A_LAYERS, N_GROUPS, GROUP_SIZE, STATE_DIM), jnp.float32) * STATE_DIM ** -0.5
    inp["ssm_c_im"] = nrm(ks[17], (N_A_LAYERS, N_GROUPS, GROUP_SIZE, STATE_DIM), jnp.float32) * STATE_DIM ** -0.5
    inp["ssm_d"] = nrm(ks[18], (N_A_LAYERS, D_MODEL), jnp.float32)
    inp["glu_w_a"] = nrm(ks[19], (N_A_LAYERS, D_MODEL, D_MODEL), jnp.float32) * D_MODEL ** -0.5
    inp["glu_w_b"] = nrm(ks[20], (N_A_LAYERS, D_MODEL, D_MODEL), jnp.float32) * D_MODEL ** -0.5
    inp["w_q"] = nrm(ks[21], (N_B_LAYERS, D_MODEL, hd2), jnp.float32) * D_MODEL ** -0.5
    inp["w_k"] = nrm(ks[22], (D_MODEL, hd2), jnp.float32) * D_MODEL ** -0.5
    inp["w_v"] = nrm(ks[23], (D_MODEL, hd2), jnp.float32) * D_MODEL ** -0.5
    inp["lambda_q1"] = 0.1 * nrm(ks[24], (N_B_LAYERS, HEAD_DIM), jnp.float32)
    inp["lambda_k1"] = 0.1 * nrm(ks[25], (N_B_LAYERS, HEAD_DIM), jnp.float32)
    inp["lambda_q2"] = 0.1 * nrm(ks[26], (N_B_LAYERS, HEAD_DIM), jnp.float32)
    inp["lambda_k2"] = 0.1 * nrm(ks[27], (N_B_LAYERS, HEAD_DIM), jnp.float32)
    inp["subln"] = 1.0 + 0.01 * nrm(ks[28], (N_B_LAYERS, 2 * HEAD_DIM), jnp.float32)
    inp["w_o"] = nrm(ks[29], (N_B_LAYERS, hd2, D_MODEL), jnp.float32) * hd2 ** -0.5
    inp["ffn_w_gate"] = nrm(ks[30], (DEPTH, D_MODEL, D_FF), jnp.float32) * D_MODEL ** -0.5
    inp["ffn_w_up"] = nrm(ks[31], (DEPTH, D_MODEL, D_FF), jnp.float32) * D_MODEL ** -0.5
    inp["ffn_w_down"] = nrm(ks[32], (DEPTH, D_FF, D_MODEL), jnp.float32) * D_FF ** -0.5
    return inp


def reference(x_prompt, x_sample, cache_k, cache_v, state_ssm_re, state_ssm_im, page_table,
              mix_norm, ffn_norm, kv_norm, final_norm,
              ssm_a_re, ssm_a_im, ssm_log_dt, ssm_b_re, ssm_b_im, ssm_c_re, ssm_c_im, ssm_d,
              glu_w_a, glu_w_b, w_q, w_k, w_v, lambda_q1, lambda_k1, lambda_q2, lambda_k2,
              subln, w_o, ffn_w_gate, ffn_w_up, ffn_w_down):
    p = dict(mix_norm=mix_norm, ffn_norm=ffn_norm, kv_norm=kv_norm, final_norm=final_norm,
             ssm_a_re=ssm_a_re, ssm_a_im=ssm_a_im, ssm_log_dt=ssm_log_dt,
             ssm_b_re=ssm_b_re, ssm_b_im=ssm_b_im, ssm_c_re=ssm_c_re, ssm_c_im=ssm_c_im, ssm_d=ssm_d,
             glu_w_a=glu_w_a, glu_w_b=glu_w_b, w_q=w_q, w_k=w_k, w_v=w_v,
             lambda_q1=lambda_q1, lambda_k1=lambda_k1, lambda_q2=lambda_q2, lambda_k2=lambda_k2,
             subln=subln, w_o=w_o, ffn_w_gate=ffn_w_gate, ffn_w_up=ffn_w_up, ffn_w_down=ffn_w_down)

    def attend_prompt(q, k, v, pos, lam):
        bsz, seqlen = q.shape[0], q.shape[1]
        nb = seqlen // Q_BLOCK
        qb = jnp.moveaxis(q.reshape(bsz, nb, Q_BLOCK, N_HEADS, 2, HEAD_DIM), 1, 0)
        pb = pos.reshape(nb, Q_BLOCK)
        ob = lax.map(lambda a: diff_attention(a[0], k, v, a[1], pos, lam), (qb, pb))
        return jnp.moveaxis(ob, 0, 1).reshape(bsz, seqlen, N_HEADS, 2 * HEAD_DIM)

    pos_p = jnp.arange(SEQ, dtype=jnp.int32)
    h0_p = jnp.zeros((N_A_LAYERS, x_prompt.shape[0], N_GROUPS, STATE_DIM), state_ssm_re.dtype)
    y_prompt, k_prompt, v_prompt, ssm_re_prompt, ssm_im_prompt = trunk(
        x_prompt, pos_p, h0_p, h0_p, attend_prompt, p)

    dec_b = x_sample.shape[0]
    n_pages = page_table.shape[1]
    past_len = n_pages * PAGE_SIZE
    past_k = cache_k[page_table].reshape(dec_b, past_len, N_HEADS, 2, HEAD_DIM)
    past_v = cache_v[page_table].reshape(dec_b, past_len, N_HEADS, 2 * HEAD_DIM)

    def attend_sample(q, k, v, pos, lam):
        k_all = jnp.concatenate([past_k.astype(k.dtype), k], axis=1)
        v_all = jnp.concatenate([past_v.astype(v.dtype), v], axis=1)
        k_pos = jnp.arange(past_len + q.shape[1], dtype=jnp.int32)
        return diff_attention(q, k_all, v_all, pos, k_pos, lam)

    pos_s = PAST_LEN + jnp.arange(DEC_SEQ, dtype=jnp.int32)
    y_sample, k_sample, v_sample, ssm_re_sample, ssm_im_sample = trunk(
        x_sample, pos_s, state_ssm_re, state_ssm_im, attend_sample, p)

    return (y_prompt, y_sample, k_prompt, v_prompt, ssm_re_prompt, ssm_im_prompt,
            k_sample, v_sample, ssm_re_sample, ssm_im_sample)
```

```python
import functools
import math

import jax
import jax.numpy as jnp
from jax import lax
from jax.experimental import pallas as pl
from jax.experimental.pallas import tpu as pltpu

EPS = 1e-5
ROPE_THETA = 500000.0
NEG = -1e30
LANES = 128
VMEM_LIMIT = 56 * 1024 * 1024
S5_CHUNK = 64
ROW_TILE = 512
ATTN_TQ = 512
ATTN_TK = 512
DECODE_PAGES_PER_STEP = 8

_NT = (((1,), (1,)), ((), ()))


def _cparams(*sem):
    return pltpu.CompilerParams(dimension_semantics=sem, vmem_limit_bytes=VMEM_LIMIT)


def _const_spec(shape):
    nd = len(shape)
    return pl.BlockSpec(shape, lambda *_: (0,) * nd)


def _rms_scale(x):
    return lax.rsqrt(jnp.mean(x * x, axis=-1, keepdims=True) + EPS)


def _row_tile(t):
    return ROW_TILE if t % ROW_TILE == 0 else t


def _norm_cast_kernel(x_ref, g_ref, o_ref):
    x = x_ref[...]
    o_ref[...] = (x * _rms_scale(x) * g_ref[...]).astype(o_ref.dtype)


def _norm_cast(x, g):
    t, d = x.shape
    tm = _row_tile(t)
    return pl.pallas_call(
        _norm_cast_kernel,
        grid=(t // tm,),
        in_specs=[pl.BlockSpec((tm, d), lambda i: (i, 0)), _const_spec((1, d))],
        out_specs=pl.BlockSpec((tm, d), lambda i: (i, 0)),
        out_shape=jax.ShapeDtypeStruct((t, d), jnp.bfloat16),
        compiler_params=_cparams("parallel"),
        name="norm_cast",
    )(x, g.reshape(1, d))


def _s5_kernel(a2r_ref, a2i_ref, ldt_ref, bcat_ref, ccat_ref, ug_ref, us_ref, h0_ref,
               y_ref, hfin_ref, ys_ref, hs_ref,
               tab_ref, w0_ref, w1_ref, wb_ref, toe_ref, *, chunk, n_chunks, batch, hg):
    f32 = jnp.float32
    bf16 = jnp.bfloat16
    C = chunk
    K = C * hg
    half = LANES // 2
    lane = lax.broadcasted_iota(jnp.int32, (1, LANES), 1)
    sgn_mp = jnp.where(lane < half, -1.0, 1.0).astype(f32)
    swap = lambda v: pltpu.roll(v, half, axis=1)

    a_r = a2r_ref[0]
    a_i = a2i_ref[0]
    dt = jnp.exp(ldt_ref[0])
    mag = jnp.exp(dt * a_r)
    ang = dt * a_i
    ab_r = mag * jnp.cos(ang)
    ab_i = mag * jnp.sin(ang)
    den = a_r * a_r + a_i * a_i
    n_r = ab_r - 1.0
    n_i = ab_i
    f_r = (n_r * a_r + n_i * a_i) / den
    f_i = (n_i * a_r - n_r * a_i) / den
    bcat = bcat_ref[0]
    bb1 = f_r * bcat + f_i * (swap(bcat) * sgn_mp)
    bb2 = swap(bb1) * sgn_mp
    ccat = ccat_ref[0]
    cc1 = ccat * (-sgn_mp)
    cc2 = -swap(ccat)

    kk = lax.broadcasted_iota(jnp.int32, (C, 1), 0)

    def powtab(expo):
        t_r = jnp.ones((C, LANES), f32)
        t_i = jnp.zeros((C, LANES), f32)
        p_r, p_i = ab_r, ab_i
        for j in range(C.bit_length() - 1):
            bit = ((expo >> j) & 1) == 1
            m_r = t_r * p_r - t_i * p_i
            m_i = t_r * p_i + t_i * p_r
            t_r = jnp.where(bit, m_r, t_r)
            t_i = jnp.where(bit, m_i, t_i)
            p_r, p_i = p_r * p_r - p_i * p_i, 2.0 * p_r * p_i
        return t_r, t_i, p_r, p_i

    t0_r, t0_i, ac_r, ac_i = powtab(kk)
    tr_r, tr_i, _, _ = powtab(C - 1 - kk)
    tab_ref[0] = t0_r
    tab_ref[1] = t0_i
    tab_ref[2] = t0_r * ab_r - t0_i * ab_i
    tab_ref[3] = t0_r * ab_i + t0_i * ab_r
    tab_ref[4] = tr_r
    tab_ref[5] = tr_i

    for k in range(C):
        rows = pl.ds(k * hg, hg)
        w0_ref[rows, :] = tab_ref[0, k:k + 1, :] * cc1 + tab_ref[1, k:k + 1, :] * cc2
        w1_ref[rows, :] = (tab_ref[2, k:k + 1, :] * cc1 + tab_ref[3, k:k + 1, :] * cc2).astype(bf16)
        wb_ref[rows, :] = (tab_ref[4, k:k + 1, :] * bb1 + tab_ref[5, k:k + 1, :] * bb2).astype(bf16)

    kt = lax.dot_general(bb1, w0_ref[...], _NT, precision=lax.Precision.HIGHEST,
                         preferred_element_type=f32)
    col = lax.broadcasted_iota(jnp.int32, (1, K), 1)
    for s in range(C):
        shifted = kt if s == 0 else jnp.where(col >= s * hg, pltpu.roll(kt, s * hg, axis=1), 0.0)
        toe_ref[pl.ds(s * hg, hg), :] = shifted.astype(bf16)

    u = ug_ref[0]
    m_rows = batch * n_chunks
    x = jnp.dot(u, wb_ref[...], preferred_element_type=f32)
    cidx = lax.broadcasted_iota(jnp.int32, (m_rows, 1), 0) % n_chunks
    p_r, p_i = ac_r, ac_i
    sh = 1
    while sh < n_chunks:
        xs = jnp.where(cidx >= sh, pltpu.roll(x, sh, axis=0), 0.0)
        x = x + xs * p_r + swap(xs) * (p_i * sgn_mp)
        p_r, p_i = p_r * p_r - p_i * p_i, 2.0 * p_r * p_i
        sh *= 2
    for b in range(batch):
        hfin_ref[0, b:b + 1, :] = x[(b + 1) * n_chunks - 1:(b + 1) * n_chunks, :]
    hprev = jnp.where(cidx >= 1, pltpu.roll(x, 1, axis=0), 0.0).astype(bf16)
    y = jnp.dot(u, toe_ref[...], preferred_element_type=f32)
    y = y + lax.dot_general(hprev, w1_ref[...], _NT, preferred_element_type=f32)
    y_ref[0] = y

    h0 = h0_ref[0]
    bu = jnp.dot(us_ref[0], bb1.astype(bf16), preferred_element_type=f32)
    hn = h0 * ab_r + swap(h0) * (ab_i * sgn_mp) + bu
    hs_ref[0] = hn
    ys_ref[0] = lax.dot_general(hn.astype(bf16), cc1.astype(bf16), _NT, preferred_element_type=f32)


def _s5_mixer(u_p, u_s, h0_re, h0_im, a_re, a_im, log_dt, b_re, b_im, c_re, c_im, batch, seqlen):
    g, p = a_re.shape
    d = u_p.shape[1]
    hg = d // g
    assert 2 * p == LANES and seqlen % S5_CHUNK == 0 and S5_CHUNK & (S5_CHUNK - 1) == 0
    c = S5_CHUNK
    nc = seqlen // c
    m = batch * nc
    k = c * hg
    bs = u_s.shape[0]
    dup = lambda v: jnp.concatenate([v, v], axis=-1).reshape(g, 1, 2 * p)
    a2r, a2i = dup(a_re), dup(a_im)
    ldt = jnp.broadcast_to(log_dt.reshape(g, 1, 1), (g, 1, LANES))
    bcat = jnp.concatenate([jnp.swapaxes(b_re, 1, 2), jnp.swapaxes(b_im, 1, 2)], axis=-1)
    ccat = jnp.concatenate([c_re, c_im], axis=-1)
    ug = u_p.reshape(m, c, g, hg).transpose(2, 0, 1, 3).reshape(g, m, k)
    us = u_s.reshape(bs, g, hg).transpose(1, 0, 2)
    h0 = jnp.concatenate([h0_re, h0_im], axis=-1).transpose(1, 0, 2)

    grp = lambda *shape: pl.BlockSpec((1,) + shape, lambda i: (i,) + (0,) * len(shape))
    y, hfin, ys, hs = pl.pallas_call(
        functools.partial(_s5_kernel, chunk=c, n_chunks=nc, batch=batch, hg=hg),
        grid=(g,),
        in_specs=[grp(1, LANES), grp(1, LANES), grp(1, LANES), grp(hg, LANES), grp(hg, LANES),
                  grp(m, k), grp(bs, hg), grp(bs, LANES)],
        out_specs=[grp(m, k), grp(batch, LANES), grp(bs, hg), grp(bs, LANES)],
        out_shape=[jax.ShapeDtypeStruct((g, m, k), jnp.float32),
                   jax.ShapeDtypeStruct((g, batch, LANES), jnp.float32),
                   jax.ShapeDtypeStruct((g, bs, hg), jnp.float32),
                   jax.ShapeDtypeStruct((g, bs, LANES), jnp.float32)],
        scratch_shapes=[pltpu.VMEM((6, c, LANES), jnp.float32),
                        pltpu.VMEM((k, LANES), jnp.float32),
                        pltpu.VMEM((k, LANES), jnp.bfloat16),
                        pltpu.VMEM((k, LANES), jnp.bfloat16),
                        pltpu.VMEM((k, k), jnp.bfloat16)],
        compiler_params=_cparams("parallel"),
        name="s5_mixer",
    )(a2r, a2i, ldt, bcat, ccat, ug, us, h0)
    y_p = y.reshape(g, m, c, hg).transpose(1, 2, 0, 3).reshape(batch * seqlen, d)
    y_s = ys.transpose(1, 0, 2).reshape(bs, d)
    split = lambda h: (h[..., :p].transpose(1, 0, 2)[None], h[..., p:].transpose(1, 0, 2)[None])
    return y_p, y_s, split(hfin), split(hs)


def _ff_chunk(d_ff):
    for n in (2, 1, 4, 11, 22):
        if d_ff % n == 0 and (d_ff // n) % LANES == 0:
            return d_ff // n
    return d_ff


def _tail_ffn_kernel(*refs, mode, final, ff_chunk):
    f32 = jnp.float32
    bf16 = jnp.bfloat16
    refs = list(refs)
    o_ref = refs.pop()
    x_ref = refs.pop(0)
    x = x_ref[...]
    if mode == "glu":
        ys_ref, gm_ref, d_ref, wa_ref, wb_ref = refs[:5]
        refs = refs[5:]
        un = x * _rms_scale(x) * gm_ref[...]
        z = jax.nn.gelu(ys_ref[...] + d_ref[...] * un).astype(bf16)
        mix = (jnp.dot(z, wa_ref[...], preferred_element_type=f32)
               * jax.nn.sigmoid(jnp.dot(z, wb_ref[...], preferred_element_type=f32)))
    else:
        o_in_ref, wo_ref = refs[:2]
        refs = refs[2:]
        mix = jnp.dot(o_in_ref[...], wo_ref[...], preferred_element_type=f32)
    gf_ref, wg_ref, wu_ref, wd_ref = refs[:4]
    x1 = x + mix
    hn = (x1 * _rms_scale(x1) * gf_ref[...]).astype(bf16)
    d_ff = wg_ref.shape[1]
    acc = x1
    for c0 in range(0, d_ff, ff_chunk):
        cols = pl.ds(c0, ff_chunk)
        gate = jnp.dot(hn, wg_ref[:, cols], preferred_element_type=f32)
        up = jnp.dot(hn, wu_ref[:, cols], preferred_element_type=f32)
        act = (jax.nn.silu(gate) * up).astype(bf16)
        acc = acc + jnp.dot(act, wd_ref[cols, :], preferred_element_type=f32)
    if final:
        acc = acc * _rms_scale(acc) * refs[4][...]
    o_ref[...] = acc


def _tail_ffn(x, mix_in, mix_w, g_ffn, wg, wu, wd, *, mode, g_final=None):
    t, d = x.shape
    d_ff = wg.shape[1]
    tm = _row_tile(t)
    row = lambda w: pl.BlockSpec((tm, w), lambda i: (i, 0))
    res = lambda shape: pl.BlockSpec(shape, lambda i: (0,) * len(shape), pipeline_mode=pl.Buffered(1))
    if mode == "glu":
        ys, = mix_in
        g_mix, dvec, wa, wb = mix_w
        args = [x, ys, g_mix.reshape(1, d), dvec.reshape(1, d), wa, wb]
        specs = [row(d), row(d), res((1, d)), res((1, d)), res(wa.shape), res(wb.shape)]
    else:
        o_in, = mix_in
        wo, = mix_w
        args = [x, o_in, wo]
        specs = [row(d), row(o_in.shape[1]), res(wo.shape)]
    args += [g_ffn.reshape(1, d), wg, wu, wd]
    specs += [res((1, d)), res(wg.shape), res(wu.shape), res(wd.shape)]
    if g_final is not None:
        args.append(g_final.reshape(1, d))
        specs.append(res((1, d)))
    return pl.pallas_call(
        functools.partial(_tail_ffn_kernel, mode=mode, final=g_final is not None, ff_chunk=_ff_chunk(d_ff)),
        grid=(t // tm,),
        in_specs=specs,
        out_specs=row(d),
        out_shape=jax.ShapeDtypeStruct((t, d), jnp.float32),
        compiler_params=_cparams("parallel"),
        name="tail_ffn_" + mode,
    )(*args)


def _rope_kernel(inv_ref, cos_ref, sa_ref, sb_ref, *, pos0, rot_dim, head_dim):
    tm = cos_ref.shape[0]
    half = rot_dim // 2
    pos = (pos0 + pl.program_id(0) * tm + lax.broadcasted_iota(jnp.int32, (tm, 1), 0)).astype(jnp.float32)
    dd = lax.broadcasted_iota(jnp.int32, (1, LANES), 1) % head_dim
    first = dd < half
    second = (dd >= half) & (dd < rot_dim)
    ang = pos * inv_ref[...]
    c = jnp.cos(ang)
    s = jnp.sin(ang)
    cos_ref[...] = jnp.where(first | second, c, 1.0)
    sa_ref[...] = jnp.where(first, -s, 0.0)
    sb_ref[...] = jnp.where(second, s, 0.0)


def _rope_tables(n_pos, pos0, head_dim):
    rot_dim = head_dim // 4
    half = rot_dim // 2
    inv = ROPE_THETA ** (-jnp.arange(0, rot_dim, 2, dtype=jnp.float32) / rot_dim)
    dd = jnp.arange(LANES) % head_dim
    inv_lane = jnp.where(dd < rot_dim, inv[dd % half], 0.0).reshape(1, LANES)
    tm = _row_tile(n_pos)
    out = jax.ShapeDtypeStruct((n_pos, LANES), jnp.float32)
    spec = pl.BlockSpec((tm, LANES), lambda i: (i, 0))
    return pl.pallas_call(
        functools.partial(_rope_kernel, pos0=pos0, rot_dim=rot_dim, head_dim=head_dim),
        grid=(n_pos // tm,),
        in_specs=[_const_spec((1, LANES))],
        out_specs=[spec, spec, spec],
        out_shape=[out, out, out],
        compiler_params=_cparams("parallel"),
        name="rope_tables",
    )(inv_lane)


def _qkv_kernel(x_ref, gq_ref, gkv_ref, wq_ref, wk_ref, wv_ref, cos_ref, sa_ref, sb_ref,
                q_ref, k_ref, kb_ref, v_ref, vb_ref, *, half, q_scale):
    f32 = jnp.float32
    bf16 = jnp.bfloat16
    x = x_ref[...]
    xr = x * _rms_scale(x)
    hq = (xr * gq_ref[...]).astype(bf16)
    hkv = (xr * gkv_ref[...]).astype(bf16)
    cos = cos_ref[...]
    sa = sa_ref[...]
    sb = sb_ref[...]

    def rope(t):
        cols = []
        for j in range(t.shape[1] // LANES):
            ts = t[:, j * LANES:(j + 1) * LANES]
            cols.append(ts * cos + pltpu.roll(ts, LANES - half, axis=1) * sa + pltpu.roll(ts, half, axis=1) * sb)
        return jnp.concatenate(cols, axis=1)

    q = rope(jnp.dot(hq, wq_ref[...], preferred_element_type=f32))
    q_ref[...] = (q * q_scale).astype(bf16)
    k = rope(jnp.dot(hkv, wk_ref[...], preferred_element_type=f32))
    k_ref[...] = k
    kb_ref[...] = k.astype(bf16)
    v = jnp.dot(hkv, wv_ref[...], preferred_element_type=f32)
    v_ref[...] = v
    vb_ref[...] = v.astype(bf16)


def _qkv(x, g_q, g_kv, wq, wk, wv, tables, head_dim, table_rows_per_seq):
    t, d = x.shape
    n = wq.shape[1]
    tm = _row_tile(t)
    tiles_per_seq = table_rows_per_seq // tm
    row = lambda w: pl.BlockSpec((tm, w), lambda i: (i, 0))
    tab = pl.BlockSpec((tm, LANES), lambda i: (i % tiles_per_seq, 0))
    f32o = jax.ShapeDtypeStruct((t, n), jnp.float32)
    bf16o = jax.ShapeDtypeStruct((t, n), jnp.bfloat16)
    return pl.pallas_call(
        functools.partial(_qkv_kernel, half=head_dim // 8, q_scale=head_dim ** -0.5),
        grid=(t // tm,),
        in_specs=[row(d), _const_spec((1, d)), _const_spec((1, d)),
                  _const_spec(wq.shape), _const_spec(wk.shape), _const_spec(wv.shape), tab, tab, tab],
        out_specs=[row(n)] * 5,
        out_shape=[bf16o, f32o, bf16o, f32o, bf16o],
        compiler_params=_cparams("parallel"),
        name="qkv_rope",
    )(x, g_q.reshape(1, d), g_kv.reshape(1, d), wq, wk, wv, *tables)


def _lambda(lam_ref, lam_init):
    l = lam_ref[...]
    return (jnp.exp(jnp.sum(l[0:1] * l[1:2], axis=-1, keepdims=True))
            - jnp.exp(jnp.sum(l[2:3] * l[3:4], axis=-1, keepdims=True)) + lam_init)


def _flash_kernel(lam_ref, sub_ref, q_ref, k_ref, v_ref, o_ref, qa_ref, qb_ref, m_ref, l_ref, acc_ref,
                  *, tq, tk, lam_init):
    f32 = jnp.float32
    bf16 = jnp.bfloat16
    qi = pl.program_id(2)
    ki = pl.program_id(3)
    last = (qi * tq + tq - 1) // tk
    half = LANES // 2

    @pl.when(ki == 0)
    def _():
        q = q_ref[0].astype(f32)
        lo = lax.broadcasted_iota(jnp.int32, (1, LANES), 1) < half
        qa_ref[...] = jnp.where(lo, q, 0.0).astype(bf16)
        qb_ref[...] = jnp.where(lo, 0.0, q).astype(bf16)
        m_ref[...] = jnp.full(m_ref.shape, -jnp.inf, f32)
        l_ref[...] = jnp.zeros(l_ref.shape, f32)
        acc_ref[...] = jnp.zeros(acc_ref.shape, f32)

    def update(masked):
        k = k_ref[0]
        v = v_ref[0]
        if masked:
            qpos = qi * tq + lax.broadcasted_iota(jnp.int32, (tq, tk), 0)
            kpos = ki * tk + lax.broadcasted_iota(jnp.int32, (tq, tk), 1)
            keep = kpos <= qpos
        for c, qc_ref in enumerate((qa_ref, qb_ref)):
            s = lax.dot_general(qc_ref[...], k, _NT, preferred_element_type=f32)
            if masked:
                s = jnp.where(keep, s, NEG)
            m_old = m_ref[c]
            m_new = jnp.maximum(m_old, jnp.max(s, axis=-1, keepdims=True))
            alpha = jnp.exp(m_old - m_new)
            p = jnp.exp(s - m_new)
            l_ref[c] = alpha * l_ref[c] + jnp.sum(p, axis=-1, keepdims=True)
            acc_ref[c] = alpha * acc_ref[c] + jnp.dot(p.astype(bf16), v, preferred_element_type=f32)
            m_ref[c] = m_new

    crosses = ki * tk + tk - 1 > qi * tq

    @pl.when((ki <= last) & crosses)
    def _():
        update(True)

    @pl.when((ki <= last) & jnp.logical_not(crosses))
    def _():
        update(False)

    @pl.when(ki == last)
    def _():
        lam = _lambda(lam_ref, lam_init)
        o = acc_ref[0] / l_ref[0] - lam * (acc_ref[1] / l_ref[1])
        o = o * _rms_scale(o) * sub_ref[...] * (1.0 - lam_init)
        o_ref[0] = o.astype(o_ref.dtype)


def _flash_attention(q, k, v, lam_vecs, subln, lam_init):
    b, l, n = q.shape
    h = n // LANES
    tq = ATTN_TQ if l % ATTN_TQ == 0 else l
    tk = ATTN_TK if l % ATTN_TK == 0 else l
    last = lambda qi: (qi * tq + tq - 1) // tk
    q_spec = pl.BlockSpec((1, tq, LANES), lambda bi, hi, qi, ki: (bi, qi, hi))
    kv_spec = pl.BlockSpec((1, tk, LANES), lambda bi, hi, qi, ki: (bi, jnp.minimum(ki, last(qi)), hi))
    return pl.pallas_call(
        functools.partial(_flash_kernel, tq=tq, tk=tk, lam_init=lam_init),
        grid=(b, h, l // tq, l // tk),
        in_specs=[_const_spec(lam_vecs.shape), _const_spec((1, LANES)), q_spec, kv_spec, kv_spec],
        out_specs=q_spec,
        out_shape=jax.ShapeDtypeStruct((b, l, n), jnp.bfloat16),
        scratch_shapes=[pltpu.VMEM((tq, LANES), jnp.bfloat16), pltpu.VMEM((tq, LANES), jnp.bfloat16),
                        pltpu.VMEM((2, tq, 1), jnp.float32), pltpu.VMEM((2, tq, 1), jnp.float32),
                        pltpu.VMEM((2, tq, LANES), jnp.float32)],
        compiler_params=_cparams("parallel", "parallel", "parallel", "arbitrary"),
        name="flash_diff_attn",
    )(lam_vecs, subln.reshape(1, LANES), q, k, v)


def _decode_kernel(pt_ref, lam_ref, sub_ref, q_ref, kc_ref, vc_ref, *refs, n_pp, head_dim, lam_init):
    f32 = jnp.float32
    bf16 = jnp.bfloat16
    k_refs = refs[:n_pp]
    v_refs = refs[n_pp:2 * n_pp]
    o_ref, qm_ref, m_ref, l_ref, acc_ref = refs[2 * n_pp:]
    j = pl.program_id(1)
    n = q_ref.shape[-1]
    r = n // head_dim
    row = lax.broadcasted_iota(jnp.int32, (r, n), 0)
    lane = lax.broadcasted_iota(jnp.int32, (r, n), 1)
    own_qk = (lane // head_dim) == row
    own_v = (lane // (2 * head_dim)) == (row // 2)

    @pl.when(j == 0)
    def _():
        qm = jnp.where(own_qk, jnp.broadcast_to(q_ref[0].astype(f32), (r, n)), 0.0)
        qm_ref[...] = qm.astype(bf16)
        s_cur = jnp.sum(qm * kc_ref[0].astype(f32), axis=-1, keepdims=True)
        m_ref[...] = s_cur
        l_ref[...] = jnp.ones(l_ref.shape, f32)
        acc_ref[...] = jnp.where(own_v, jnp.broadcast_to(vc_ref[0].astype(f32), (r, n)), 0.0)

    qm = qm_ref[...]
    s = jnp.concatenate(
        [lax.dot_general(qm, kr[0].astype(bf16), _NT, preferred_element_type=f32) for kr in k_refs], axis=1)
    m_old = m_ref[...]
    m_new = jnp.maximum(m_old, jnp.max(s, axis=-1, keepdims=True))
    alpha = jnp.exp(m_old - m_new)
    p = jnp.exp(s - m_new)
    l_ref[...] = alpha * l_ref[...] + jnp.sum(p, axis=-1, keepdims=True)
    m_ref[...] = m_new
    page = k_refs[0].shape[1]
    pv = jnp.zeros((r, n), f32)
    for i, vr in enumerate(v_refs):
        pv = pv + jnp.dot(p[:, i * page:(i + 1) * page].astype(bf16), vr[0].astype(bf16),
                          preferred_element_type=f32)
    acc_ref[...] = alpha * acc_ref[...] + pv

    @pl.when(j == pl.num_programs(1) - 1)
    def _():
        lam = _lambda(lam_ref, lam_init)
        rr = lax.broadcasted_iota(jnp.int32, (r, 1), 0)
        coef = jnp.where(rr % 2 == 0, 1.0, -lam) / l_ref[...]
        o = jnp.sum(jnp.where(own_v, acc_ref[...] * coef, 0.0), axis=0, keepdims=True)
        parts = []
        for hh in range(n // LANES):
            oh = o[:, hh * LANES:(hh + 1) * LANES]
            parts.append(oh * _rms_scale(oh) * sub_ref[...] * (1.0 - lam_init))
        o_ref[0] = jnp.concatenate(parts, axis=1).astype(o_ref.dtype)


def _decode_attention(q, k_cur, v_cur, cache_k, cache_v, page_table, lam_vecs, subln, head_dim, lam_init):
    bs, n = q.shape
    n_pages = page_table.shape[1]
    n_pp = math.gcd(DECODE_PAGES_PER_STEP, n_pages)
    page = cache_k.shape[1]
    ck = cache_k.reshape(cache_k.shape[0], page, n)
    cv = cache_v.reshape(cache_v.shape[0], page, n)
    r = n // head_dim
    vec = pl.BlockSpec((1, 1, n), lambda b, j, pt: (b, 0, 0))
    page_spec = lambda i: pl.BlockSpec((1, page, n), lambda b, j, pt: (pt[b, j * n_pp + i], 0, 0))
    grid_spec = pltpu.PrefetchScalarGridSpec(
        num_scalar_prefetch=1,
        grid=(bs, n_pages // n_pp),
        in_specs=[pl.BlockSpec(lam_vecs.shape, lambda b, j, pt: (0, 0)),
                  pl.BlockSpec((1, LANES), lambda b, j, pt: (0, 0)), vec, vec, vec]
                 + [page_spec(i) for i in range(n_pp)] * 2,
        out_specs=vec,
        scratch_shapes=[pltpu.VMEM((r, n), jnp.bfloat16), pltpu.VMEM((r, 1), jnp.float32),
                        pltpu.VMEM((r, 1), jnp.float32), pltpu.VMEM((r, n), jnp.float32)],
    )
    o = pl.pallas_call(
        functools.partial(_decode_kernel, n_pp=n_pp, head_dim=head_dim, lam_init=lam_init),
        grid_spec=grid_spec,
        out_shape=jax.ShapeDtypeStruct((bs, 1, n), jnp.bfloat16),
        compiler_params=_cparams("parallel", "arbitrary"),
        name="paged_decode_attn",
    )(page_table, lam_vecs, subln.reshape(1, LANES), q.reshape(bs, 1, n), k_cur.reshape(bs, 1, n),
      v_cur.reshape(bs, 1, n), *([ck] * n_pp), *([cv] * n_pp))
    return o.reshape(bs, n)


def kernel(x_prompt, x_sample, cache_k, cache_v, state_ssm_re, state_ssm_im, page_table, mix_norm, ffn_norm, kv_norm, final_norm, ssm_a_re, ssm_a_im, ssm_log_dt, ssm_b_re, ssm_b_im, ssm_c_re, ssm_c_im, ssm_d, glu_w_a, glu_w_b, w_q, w_k, w_v, lambda_q1, lambda_k1, lambda_q2, lambda_k2, subln, w_o, ffn_w_gate, ffn_w_up, ffn_w_down):
    bf16 = jnp.bfloat16
    b, l, d = x_prompt.shape
    bs, dec_seq, _ = x_sample.shape
    assert dec_seq == 1 and mix_norm.shape[0] == 2 and ssm_a_re.shape[0] == 1 and w_q.shape[0] == 1
    head_dim = subln.shape[1] // 2
    n_heads = w_q.shape[2] // (2 * head_dim)
    assert 2 * head_dim == LANES
    past_len = page_table.shape[1] * cache_k.shape[1]
    attn_layer = 1
    lam_init = 0.8 - 0.6 * math.exp(-0.3 * attn_layer)

    xp = x_prompt.reshape(b * l, d)
    xs = x_sample.reshape(bs, d)
    cast = lambda w: w.astype(bf16)
    wa, wb = cast(glu_w_a[0]), cast(glu_w_b[0])
    wq, wk, wv, wo = cast(w_q[0]), cast(w_k), cast(w_v), cast(w_o[0])
    wg, wu, wd = cast(ffn_w_gate), cast(ffn_w_up), cast(ffn_w_down)
    lam_vecs = jnp.concatenate([lambda_q1, lambda_k1, lambda_q2, lambda_k2], axis=0)

    up = _norm_cast(xp, mix_norm[0])
    us = _norm_cast(xs, mix_norm[0])
    y_p, y_s, (re_p, im_p), (re_s, im_s) = _s5_mixer(
        up, us, state_ssm_re[0], state_ssm_im[0], ssm_a_re[0], ssm_a_im[0], ssm_log_dt[0],
        ssm_b_re[0], ssm_b_im[0], ssm_c_re[0], ssm_c_im[0], b, l)
    glu_w = (mix_norm[0], ssm_d[0], wa, wb)
    x2p = _tail_ffn(xp, (y_p,), glu_w, ffn_norm[0], wg[0], wu[0], wd[0], mode="glu")
    x2s = _tail_ffn(xs, (y_s,), glu_w, ffn_norm[0], wg[0], wu[0], wd[0], mode="glu")

    tab_p = _rope_tables(l, 0, head_dim)
    tab_s = _rope_tables(8, past_len, head_dim)
    tab_s = [jnp.broadcast_to(t[:1], (bs, LANES)) for t in tab_s]
    q_p, k_p, kb_p, v_p, vb_p = _qkv(x2p, mix_norm[1], kv_norm, wq, wk, wv, tab_p, head_dim, l)
    q_s, k_s, kb_s, v_s, vb_s = _qkv(x2s, mix_norm[1], kv_norm, wq, wk, wv, tab_s, head_dim, bs)

    n = q_p.shape[1]
    o_p = _flash_attention(q_p.reshape(b, l, n), kb_p.reshape(b, l, n), vb_p.reshape(b, l, n),
                           lam_vecs, subln[0], lam_init).reshape(b * l, n)
    o_s = _decode_attention(q_s, kb_s, vb_s, cache_k, cache_v, page_table, lam_vecs, subln[0],
                            head_dim, lam_init)
    y_prompt = _tail_ffn(x2p, (o_p,), (wo,), ffn_norm[1], wg[1], wu[1], wd[1], mode="attn", g_final=final_norm)
    y_sample = _tail_ffn(x2s, (o_s,), (wo,), ffn_norm[1], wg[1], wu[1], wd[1], mode="attn", g_final=final_norm)

    return (y_prompt.reshape(b, l, d), y_sample.reshape(bs, 1, d),
            k_p.reshape(b, l, n_heads, 2, head_dim), v_p.reshape(b, l, n_heads, 2 * head_dim),
            re_p, im_p,
            k_s.reshape(bs, 1, n_heads, 2, head_dim), v_s.reshape(bs, 1, n_heads, 2 * head_dim),
            re_s, im_s)
```

```python
import functools
import math

import numpy as np

import jax
import jax.numpy as jnp
from jax import lax
from jax.experimental import pallas as pl
from jax.experimental.pallas import tpu as pltpu

EPS = 1e-5
ROPE_THETA = 500000.0
NEG = -1e30
LANES = 128
SUBLANES = 8
VMEM_LIMIT = 56 * 1024 * 1024
S5_CHUNK = 64
ROW_TILE = 512
ATTN_TQ = 1024
ATTN_TK = 512
ATTN_QB = 1024
DECODE_PAGES_PER_STEP = 8

_NT = (((1,), (1,)), ((), ()))


def _cparams(*sem):
    return pltpu.CompilerParams(dimension_semantics=sem, vmem_limit_bytes=VMEM_LIMIT)


def _const_spec(shape):
    nd = len(shape)
    return pl.BlockSpec(shape, lambda *_: (0,) * nd)


def _rms_scale(x):
    return lax.rsqrt(jnp.mean(x * x, axis=-1, keepdims=True) + EPS)


def _row_tile(t):
    return ROW_TILE if t % ROW_TILE == 0 else t


def _norm_cast_kernel(x_ref, g_ref, o_ref):
    x = x_ref[...]
    o_ref[...] = (x * _rms_scale(x) * g_ref[...]).astype(o_ref.dtype)


def _norm_cast(x, g):
    t, d = x.shape
    tm = _row_tile(t)
    return pl.pallas_call(
        _norm_cast_kernel,
        grid=(t // tm,),
        in_specs=[pl.BlockSpec((tm, d), lambda i: (i, 0)), _const_spec((1, d))],
        out_specs=pl.BlockSpec((tm, d), lambda i: (i, 0)),
        out_shape=jax.ShapeDtypeStruct((t, d), jnp.bfloat16),
        compiler_params=_cparams("parallel"),
        name="norm_cast",
    )(x, g.reshape(1, d))


def _norm_stepmajor_kernel(x_ref, g_ref, o_ref, buf_ref, *, chunk):
    x = x_ref[...]
    xn = x * _rms_scale(x) * g_ref[...]
    n_chunks = x.shape[0] // chunk
    for j in range(x.shape[1] // LANES):
        cols = slice(j * LANES, (j + 1) * LANES)
        buf_ref[j] = xn[:, cols]
        for s in range(chunk):
            o_ref[s, :, cols] = buf_ref[j, pl.ds(s, n_chunks, stride=chunk), :].astype(o_ref.dtype)


def _norm_stepmajor(x, g, chunk):
    t, d = x.shape
    chunks_per_tile = 2 * SUBLANES
    tm = chunks_per_tile * chunk
    assert t % tm == 0
    return pl.pallas_call(
        functools.partial(_norm_stepmajor_kernel, chunk=chunk),
        grid=(t // tm,),
        in_specs=[pl.BlockSpec((tm, d), lambda i: (i, 0)), _const_spec((1, d))],
        out_specs=pl.BlockSpec((chunk, chunks_per_tile, d), lambda i: (0, i, 0)),
        out_shape=jax.ShapeDtypeStruct((chunk, t // chunk, d), jnp.bfloat16),
        scratch_shapes=[pltpu.VMEM((d // LANES, tm, LANES), jnp.float32)],
        compiler_params=_cparams("parallel"),
        name="norm_stepmajor",
    )(x, g.reshape(1, d))


def _s5_kernel(a2r_ref, a2i_ref, ldt_ref, bcat_ref, ccat_ref, ug_ref, us_ref, h0_ref,
               y_ref, hfin_ref, ys_ref, hs_ref,
               u_ref, tab_ref, w0_ref, w1_ref, wb_ref, toe_ref, *, chunk, n_chunks, batch, hg):
    f32 = jnp.float32
    bf16 = jnp.bfloat16
    C = chunk
    K = C * hg
    gpt = LANES // hg
    half = LANES // 2
    gi = pl.program_id(1)
    lane = lax.broadcasted_iota(jnp.int32, (1, LANES), 1)
    sgn_mp = jnp.where(lane < half, -1.0, 1.0).astype(f32)
    swap = lambda v: pltpu.roll(v, half, axis=1)

    for s8 in range(C // gpt):
        tile = None
        for ss in range(gpt):
            xs = ug_ref[s8 * gpt + ss].astype(f32)
            moved = pltpu.roll(xs, ((ss - gi) * hg) & (LANES - 1), axis=1)
            moved = jnp.where(lane // hg == ss, moved, 0.0)
            tile = moved if tile is None else tile + moved
        u_ref[:, s8 * LANES:(s8 + 1) * LANES] = tile.astype(bf16)

    a_r = a2r_ref[0]
    a_i = a2i_ref[0]
    dt = jnp.exp(ldt_ref[0])
    mag = jnp.exp(dt * a_r)
    ang = dt * a_i
    ab_r = mag * jnp.cos(ang)
    ab_i = mag * jnp.sin(ang)
    den = a_r * a_r + a_i * a_i
    n_r = ab_r - 1.0
    n_i = ab_i
    f_r = (n_r * a_r + n_i * a_i) / den
    f_i = (n_i * a_r - n_r * a_i) / den
    bcat = bcat_ref[0]
    bb1 = f_r * bcat + f_i * (swap(bcat) * sgn_mp)
    bb2 = swap(bb1) * sgn_mp
    ccat = ccat_ref[0]
    cc1 = ccat * (-sgn_mp)
    cc2 = -swap(ccat)

    kk = lax.broadcasted_iota(jnp.int32, (C, 1), 0)

    def powtab(expo):
        t_r = jnp.ones((C, LANES), f32)
        t_i = jnp.zeros((C, LANES), f32)
        p_r, p_i = ab_r, ab_i
        for j in range(C.bit_length() - 1):
            bit = ((expo >> j) & 1) == 1
            m_r = t_r * p_r - t_i * p_i
            m_i = t_r * p_i + t_i * p_r
            t_r = jnp.where(bit, m_r, t_r)
            t_i = jnp.where(bit, m_i, t_i)
            p_r, p_i = p_r * p_r - p_i * p_i, 2.0 * p_r * p_i
        return t_r, t_i, p_r, p_i

    t0_r, t0_i, ac_r, ac_i = powtab(kk)
    tr_r, tr_i, _, _ = powtab(C - 1 - kk)
    tab_ref[0] = t0_r
    tab_ref[1] = t0_i
    tab_ref[2] = t0_r * ab_r - t0_i * ab_i
    tab_ref[3] = t0_r * ab_i + t0_i * ab_r
    tab_ref[4] = tr_r
    tab_ref[5] = tr_i

    for k in range(C):
        rows = pl.ds(k * hg, hg)
        w0_ref[rows, :] = tab_ref[0, k:k + 1, :] * cc1 + tab_ref[1, k:k + 1, :] * cc2
        w1_ref[rows, :] = (tab_ref[2, k:k + 1, :] * cc1 + tab_ref[3, k:k + 1, :] * cc2).astype(bf16)
        wb_ref[rows, :] = (tab_ref[4, k:k + 1, :] * bb1 + tab_ref[5, k:k + 1, :] * bb2).astype(bf16)

    kt = lax.dot_general(bb1, w0_ref[...], _NT, precision=lax.Precision.HIGHEST,
                         preferred_element_type=f32)
    col = lax.broadcasted_iota(jnp.int32, (1, K), 1)
    for s in range(C):
        shifted = kt if s == 0 else jnp.where(col >= s * hg, pltpu.roll(kt, s * hg, axis=1), 0.0)
        toe_ref[pl.ds(s * hg, hg), :] = shifted.astype(bf16)

    u = u_ref[...]
    m_rows = batch * n_chunks
    x = jnp.dot(u, wb_ref[...], preferred_element_type=f32)
    cidx = lax.broadcasted_iota(jnp.int32, (m_rows, 1), 0) % n_chunks
    p_r, p_i = ac_r, ac_i
    sh = 1
    while sh < n_chunks:
        xs = jnp.where(cidx >= sh, pltpu.roll(x, sh, axis=0), 0.0)
        x = x + xs * p_r + swap(xs) * (p_i * sgn_mp)
        p_r, p_i = p_r * p_r - p_i * p_i, 2.0 * p_r * p_i
        sh *= 2
    for b in range(batch):
        hfin_ref[0, b:b + 1, :] = x[(b + 1) * n_chunks - 1:(b + 1) * n_chunks, :]
    hprev = jnp.where(cidx >= 1, pltpu.roll(x, 1, axis=0), 0.0).astype(bf16)
    y = jnp.dot(u, toe_ref[...], preferred_element_type=f32)
    y = y + lax.dot_general(hprev, w1_ref[...], _NT, preferred_element_type=f32)

    own = jnp.broadcast_to((lane // hg) == gi, (m_rows, LANES))
    for t8 in range(C // gpt):
        tile = y[:, t8 * LANES:(t8 + 1) * LANES]
        for tt in range(gpt):
            moved = pltpu.roll(tile, ((gi - tt) * hg) & (LANES - 1), axis=1)
            pltpu.store(y_ref.at[t8 * gpt + tt], moved, mask=own)

    h0 = h0_ref[0]
    bu = jnp.dot(us_ref[0], bb1.astype(bf16), preferred_element_type=f32)
    hn = h0 * ab_r + swap(h0) * (ab_i * sgn_mp) + bu
    hs_ref[0] = hn
    ys_ref[0] = lax.dot_general(hn.astype(bf16), cc1.astype(bf16), _NT, preferred_element_type=f32)


def _s5_mixer(u_sm, u_s, h0_re, h0_im, a_re, a_im, log_dt, b_re, b_im, c_re, c_im, batch, seqlen):
    g, p = a_re.shape
    c, m, d = u_sm.shape
    hg = d // g
    gpt = LANES // hg
    assert 2 * p == LANES and c == S5_CHUNK and c & (c - 1) == 0 and c % gpt == 0 and g % gpt == 0
    nc = seqlen // c
    assert m == batch * nc
    k = c * hg
    bs = u_s.shape[0]
    dup = lambda v: jnp.concatenate([v, v], axis=-1).reshape(g, 1, 2 * p)
    a2r, a2i = dup(a_re), dup(a_im)
    ldt = jnp.broadcast_to(log_dt.reshape(g, 1, 1), (g, 1, LANES))
    bcat = jnp.concatenate([jnp.swapaxes(b_re, 1, 2), jnp.swapaxes(b_im, 1, 2)], axis=-1)
    ccat = jnp.concatenate([c_re, c_im], axis=-1)
    us = u_s.reshape(bs, g, hg).transpose(1, 0, 2)
    h0 = jnp.concatenate([h0_re, h0_im], axis=-1).transpose(1, 0, 2)

    grp = lambda *shape: pl.BlockSpec((1,) + shape, lambda j, i: (j * gpt + i,) + (0,) * len(shape))
    tile = pl.BlockSpec((c, m, LANES), lambda j, i: (0, 0, j))
    y, hfin, ys, hs = pl.pallas_call(
        functools.partial(_s5_kernel, chunk=c, n_chunks=nc, batch=batch, hg=hg),
        grid=(g // gpt, gpt),
        in_specs=[grp(1, LANES), grp(1, LANES), grp(1, LANES), grp(hg, LANES), grp(hg, LANES),
                  tile, grp(bs, hg), grp(bs, LANES)],
        out_specs=[tile, grp(batch, LANES), grp(bs, hg), grp(bs, LANES)],
        out_shape=[jax.ShapeDtypeStruct((c, m, d), jnp.float32),
                   jax.ShapeDtypeStruct((g, batch, LANES), jnp.float32),
                   jax.ShapeDtypeStruct((g, bs, hg), jnp.float32),
                   jax.ShapeDtypeStruct((g, bs, LANES), jnp.float32)],
        scratch_shapes=[pltpu.VMEM((m, k), jnp.bfloat16),
                        pltpu.VMEM((6, c, LANES), jnp.float32),
                        pltpu.VMEM((k, LANES), jnp.float32),
                        pltpu.VMEM((k, LANES), jnp.bfloat16),
                        pltpu.VMEM((k, LANES), jnp.bfloat16),
                        pltpu.VMEM((k, k), jnp.bfloat16)],
        compiler_params=_cparams("parallel", "arbitrary"),
        name="s5_mixer",
    )(a2r, a2i, ldt, bcat, ccat, u_sm, us, h0)
    y_s = ys.transpose(1, 0, 2).reshape(bs, d)
    split = lambda h: (h[..., :p].transpose(1, 0, 2)[None], h[..., p:].transpose(1, 0, 2)[None])
    return y, y_s, split(hfin), split(hs)


def _ff_chunk(d_ff):
    for n in (2, 1, 4, 11, 22):
        if d_ff % n == 0 and (d_ff // n) % LANES == 0:
            return d_ff // n
    return d_ff


def _tail_ffn_kernel(*refs, mode, final, ff_chunk, step_major):
    f32 = jnp.float32
    bf16 = jnp.bfloat16
    refs = list(refs)
    if step_major:
        ybuf_ref = refs.pop()
    o_ref = refs.pop()
    x_ref = refs.pop(0)
    x = x_ref[...]
    if mode == "glu":
        ys_ref, gm_ref, d_ref, wa_ref, wb_ref = refs[:5]
        refs = refs[5:]
        if step_major:
            chunk, per_tile = ys_ref.shape[0], ys_ref.shape[1]
            parts = []
            for j in range(ys_ref.shape[2] // LANES):
                for s in range(chunk):
                    ybuf_ref[j, pl.ds(s, per_tile, stride=chunk), :] = ys_ref[s, :, j * LANES:(j + 1) * LANES]
                parts.append(ybuf_ref[j])
            ys = jnp.concatenate(parts, axis=1)
        else:
            ys = ys_ref[...]
        un = x * _rms_scale(x) * gm_ref[...]
        z = jax.nn.gelu(ys + d_ref[...] * un).astype(bf16)
        mix = (jnp.dot(z, wa_ref[...], preferred_element_type=f32)
               * jax.nn.sigmoid(jnp.dot(z, wb_ref[...], preferred_element_type=f32)))
    else:
        o_in_ref, wo_ref = refs[:2]
        refs = refs[2:]
        mix = jnp.dot(o_in_ref[...], wo_ref[...], preferred_element_type=f32)
    gf_ref, wg_ref, wu_ref, wd_ref = refs[:4]
    x1 = x + mix
    hn = (x1 * _rms_scale(x1) * gf_ref[...]).astype(bf16)
    d_ff = wg_ref.shape[1]
    acc = x1
    for c0 in range(0, d_ff, ff_chunk):
        cols = pl.ds(c0, ff_chunk)
        gate = jnp.dot(hn, wg_ref[:, cols], preferred_element_type=f32)
        up = jnp.dot(hn, wu_ref[:, cols], preferred_element_type=f32)
        act = (jax.nn.silu(gate) * up).astype(bf16)
        acc = acc + jnp.dot(act, wd_ref[cols, :], preferred_element_type=f32)
    if final:
        acc = acc * _rms_scale(acc) * refs[4][...]
    o_ref[...] = acc


def _tail_ffn(x, mix_in, mix_w, g_ffn, wg, wu, wd, *, mode, g_final=None, step_major=False):
    t, d = x.shape
    d_ff = wg.shape[1]
    tm = _row_tile(t)
    row = lambda w: pl.BlockSpec((tm, w), lambda i: (i, 0))
    res = lambda shape: pl.BlockSpec(shape, lambda i: (0,) * len(shape), pipeline_mode=pl.Buffered(1))
    scratch = []
    if mode == "glu":
        ys, = mix_in
        g_mix, dvec, wa, wb = mix_w
        if step_major:
            chunk = ys.shape[0]
            assert tm % (chunk * SUBLANES) == 0
            ys_spec = pl.BlockSpec((chunk, tm // chunk, d), lambda i: (0, i, 0))
            scratch = [pltpu.VMEM((d // LANES, tm, LANES), jnp.float32)]
        else:
            ys_spec = row(d)
        args = [x, ys, g_mix.reshape(1, d), dvec.reshape(1, d), wa, wb]
        specs = [row(d), ys_spec, res((1, d)), res((1, d)), res(wa.shape), res(wb.shape)]
    else:
        o_in, = mix_in
        wo, = mix_w
        args = [x, o_in, wo]
        specs = [row(d), row(o_in.shape[1]), res(wo.shape)]
    args += [g_ffn.reshape(1, d), wg, wu, wd]
    specs += [res((1, d)), res(wg.shape), res(wu.shape), res(wd.shape)]
    if g_final is not None:
        args.append(g_final.reshape(1, d))
        specs.append(res((1, d)))
    return pl.pallas_call(
        functools.partial(_tail_ffn_kernel, mode=mode, final=g_final is not None, ff_chunk=_ff_chunk(d_ff),
                          step_major=step_major),
        grid=(t // tm,),
        in_specs=specs,
        out_specs=row(d),
        out_shape=jax.ShapeDtypeStruct((t, d), jnp.float32),
        scratch_shapes=scratch,
        compiler_params=_cparams("parallel"),
        name="tail_ffn_" + mode,
    )(*args)


def _rope_inv(rot_dim):
    return ROPE_THETA ** (-jnp.arange(0, rot_dim, 2, dtype=jnp.float32) / rot_dim)


def _rope_kernel(inv_ref, cos_ref, sa_ref, sb_ref, *, pos0, rot_dim, head_dim):
    tm = cos_ref.shape[0]
    half = rot_dim // 2
    pos = (pos0 + pl.program_id(0) * tm + lax.broadcasted_iota(jnp.int32, (tm, 1), 0)).astype(jnp.float32)
    dd = lax.broadcasted_iota(jnp.int32, (1, LANES), 1) % head_dim
    first = dd < half
    second = (dd >= half) & (dd < rot_dim)
    ang = pos * inv_ref[...]
    c = jnp.cos(ang)
    s = jnp.sin(ang)
    cos_ref[...] = jnp.where(first | second, c, 1.0)
    sa_ref[...] = jnp.where(first, -s, 0.0)
    sb_ref[...] = jnp.where(second, s, 0.0)


def _rope_tables(n_pos, pos0, head_dim):
    rot_dim = head_dim // 4
    half = rot_dim // 2
    dd = jnp.arange(LANES) % head_dim
    inv_lane = jnp.where(dd < rot_dim, _rope_inv(rot_dim)[dd % half], 0.0).reshape(1, LANES)
    tm = _row_tile(n_pos)
    out = jax.ShapeDtypeStruct((n_pos, LANES), jnp.float32)
    spec = pl.BlockSpec((tm, LANES), lambda i: (i, 0))
    return pl.pallas_call(
        functools.partial(_rope_kernel, pos0=pos0, rot_dim=rot_dim, head_dim=head_dim),
        grid=(n_pos // tm,),
        in_specs=[_const_spec((1, LANES))],
        out_specs=[spec, spec, spec],
        out_shape=[out, out, out],
        compiler_params=_cparams("parallel"),
        name="rope_tables",
    )(inv_lane)


def _rope_cols_kernel(inv_ref, cos_ref, sin_ref):
    n = cos_ref.shape[1]
    pos = lax.broadcasted_iota(jnp.int32, (1, n), 1).astype(jnp.float32)
    ang = inv_ref[...] * pos
    cos_ref[...] = jnp.cos(ang)
    sin_ref[...] = jnp.sin(ang)


def _rope_tables_cols(n_pos, head_dim):
    half = head_dim // 8
    out = jax.ShapeDtypeStruct((half, n_pos), jnp.float32)
    return pl.pallas_call(
        _rope_cols_kernel,
        out_shape=[out, out],
        name="rope_tables_cols",
    )(_rope_inv(2 * half).reshape(half, 1))


def _rope_lanes(t, cos, sa, sb, half):
    cols = []
    for j in range(t.shape[1] // LANES):
        ts = t[:, j * LANES:(j + 1) * LANES]
        cols.append(ts * cos + pltpu.roll(ts, LANES - half, axis=1) * sa + pltpu.roll(ts, half, axis=1) * sb)
    return jnp.concatenate(cols, axis=1)


def _qkv_kernel(x_ref, gq_ref, gkv_ref, wq_ref, wk_ref, wv_ref, cos_ref, sa_ref, sb_ref, *rest,
                half, head_dim, q_scale, transposed):
    f32 = jnp.float32
    bf16 = jnp.bfloat16
    if transposed:
        wvt_ref, cosc_ref, sinc_ref, q_ref, k_ref, kb_ref, v_ref, vb_ref = rest
    else:
        q_ref, k_ref, kb_ref, v_ref, vb_ref = rest
    x = x_ref[...]
    xr = x * _rms_scale(x)
    hq = (xr * gq_ref[...]).astype(bf16)
    hkv = (xr * gkv_ref[...]).astype(bf16)
    cos = cos_ref[...]
    sa = sa_ref[...]
    sb = sb_ref[...]
    k = _rope_lanes(jnp.dot(hkv, wk_ref[...], preferred_element_type=f32), cos, sa, sb, half)
    k_ref[...] = k
    kb_ref[...] = k.astype(bf16)
    if transposed:
        v_ref[...] = jnp.dot(hkv, wv_ref[...], preferred_element_type=f32)
        vb_ref[...] = lax.dot_general(wvt_ref[...], hkv, _NT, preferred_element_type=f32).astype(bf16)
        qt = lax.dot_general(wq_ref[...], hq, _NT, preferred_element_type=f32)
        cosc = cosc_ref[...]
        sinc = sinc_ref[...]
        for r0 in range(0, qt.shape[0], head_dim):
            x1 = qt[r0:r0 + half]
            x2 = qt[r0 + half:r0 + 2 * half]
            rot = jnp.concatenate([x1 * cosc - x2 * sinc, x2 * cosc + x1 * sinc], axis=0)
            q_ref[r0:r0 + 2 * half, :] = (rot * q_scale).astype(bf16)
            q_ref[r0 + 2 * half:r0 + head_dim, :] = (qt[r0 + 2 * half:r0 + head_dim] * q_scale).astype(bf16)
    else:
        v = jnp.dot(hkv, wv_ref[...], preferred_element_type=f32)
        v_ref[...] = v
        vb_ref[...] = v.astype(bf16)
        q = _rope_lanes(jnp.dot(hq, wq_ref[...], preferred_element_type=f32), cos, sa, sb, half)
        q_ref[...] = (q * q_scale).astype(bf16)


def _qkv(x, g_q, g_kv, wq, wk, wv, tables, head_dim, q_scale, seqlen=None, col_tables=None):
    t, d = x.shape
    n = wk.shape[1]
    tm = _row_tile(t)
    transposed = col_tables is not None
    half = head_dim // 8
    assert half == SUBLANES
    tiles_per_seq = (seqlen or t) // tm
    row = lambda w: pl.BlockSpec((tm, w), lambda i: (i, 0))
    col = lambda h: pl.BlockSpec((h, tm), lambda i: (0, i))
    tab = pl.BlockSpec((tm, LANES), lambda i: (i % tiles_per_seq, 0))
    f32o = jax.ShapeDtypeStruct((t, n), jnp.float32)
    bf16o = jax.ShapeDtypeStruct((t, n), jnp.bfloat16)
    bf16t = jax.ShapeDtypeStruct((n, t), jnp.bfloat16)
    specs = [row(d), _const_spec((1, d)), _const_spec((1, d)),
             _const_spec(wq.shape), _const_spec(wk.shape), _const_spec(wv.shape), tab, tab, tab]
    args = [x, g_q.reshape(1, d), g_kv.reshape(1, d), wq, wk, wv, *tables]
    if transposed:
        ctab = pl.BlockSpec((half, tm), lambda i: (0, i % tiles_per_seq))
        specs += [_const_spec((n, d)), ctab, ctab]
        args += [wv.T, *col_tables]
        args[3] = wq.T
        specs[3] = _const_spec((n, d))
        out_specs = [col(n), row(n), row(n), row(n), col(n)]
        out_shape = [bf16t, f32o, bf16o, f32o, bf16t]
    else:
        out_specs = [row(n)] * 5
        out_shape = [bf16o, f32o, bf16o, f32o, bf16o]
    return pl.pallas_call(
        functools.partial(_qkv_kernel, half=half, head_dim=head_dim, q_scale=q_scale, transposed=transposed),
        grid=(t // tm,),
        in_specs=specs,
        out_specs=out_specs,
        out_shape=out_shape,
        compiler_params=_cparams("parallel"),
        name="qkv_rope_t" if transposed else "qkv_rope",
    )(*args)


def _lambda(lam_ref, lam_init):
    l = lam_ref[...]
    return (jnp.exp(jnp.sum(l[0:1] * l[1:2], axis=-1, keepdims=True))
            - jnp.exp(jnp.sum(l[2:3] * l[3:4], axis=-1, keepdims=True)) + lam_init)


def _flash_kernel(qtab_ref, ktab_ref, lam_ref, sub_ref, q_ref, k_ref, v_ref, o_ref,
                  qa_ref, qb_ref, m_ref, l_ref, acc_ref, *, tq, tk, qb, lam_init):
    f32 = jnp.float32
    bf16 = jnp.bfloat16
    step = pl.program_id(2)
    qi = qtab_ref[step]
    ki = ktab_ref[step]
    last = (qi * tq + tq - 1) // tk
    half = LANES // 2

    @pl.when(ki == 0)
    def _():
        q = q_ref[...].astype(f32)
        lo = lax.broadcasted_iota(jnp.int32, (LANES, 1), 0) < half
        qa_ref[...] = jnp.where(lo, q, 0.0).astype(bf16)
        qb_ref[...] = jnp.where(lo, 0.0, q).astype(bf16)
        m_ref[...] = jnp.full(m_ref.shape, -jnp.inf, f32)
        l_ref[...] = jnp.zeros(l_ref.shape, f32)
        acc_ref[...] = jnp.zeros(acc_ref.shape, f32)

    def update(masked):
        k = k_ref[...]
        vt = v_ref[...]
        for j in range(tq // qb):
            cols = slice(j * qb, (j + 1) * qb)
            if masked:
                kpos = ki * tk + lax.broadcasted_iota(jnp.int32, (tk, 1), 0)
                qpos = qi * tq + j * qb + lax.broadcasted_iota(jnp.int32, (1, qb), 1)
                keep = kpos <= qpos
            for c, qc_ref in enumerate((qa_ref, qb_ref)):
                s = jnp.dot(k, qc_ref[:, cols], preferred_element_type=f32)
                if masked:
                    s = jnp.where(keep, s, NEG)
                m_old = m_ref[c, :, cols]
                m_new = jnp.maximum(m_old, jnp.max(s, axis=0, keepdims=True))
                alpha = jnp.exp2(m_old - m_new)
                p = jnp.exp2(s - m_new)
                l_ref[c, :, cols] = alpha * l_ref[c, :, cols] + jnp.sum(p, axis=0, keepdims=True)
                acc_ref[c, :, cols] = alpha * acc_ref[c, :, cols] + jnp.dot(
                    vt, p.astype(bf16), preferred_element_type=f32)
                m_ref[c, :, cols] = m_new

    crosses = ki * tk + tk - 1 > qi * tq

    @pl.when(crosses)
    def _():
        update(True)

    @pl.when(jnp.logical_not(crosses))
    def _():
        update(False)

    @pl.when(ki == last)
    def _():
        lam = _lambda(lam_ref, lam_init)
        ot = acc_ref[0] * (1.0 / l_ref[0]) - lam * (acc_ref[1] * (1.0 / l_ref[1]))
        r = lax.rsqrt(jnp.mean(ot * ot, axis=0, keepdims=True) + EPS)
        ot = ot * r * sub_ref[...] * (1.0 - lam_init)
        o_ref[...] = ot.T.astype(o_ref.dtype)


def _flash_attention(qt, k, vt, lam_vecs, subln, lam_init, batch, seqlen):
    n, t = qt.shape
    h = n // LANES
    tq = ATTN_TQ if seqlen % ATTN_TQ == 0 else seqlen
    tk = ATTN_TK if seqlen % ATTN_TK == 0 else seqlen
    nq, nk = seqlen // tq, seqlen // tk
    pairs = [(qi, ki) for qi in range(nq) for ki in range((qi * tq + tq - 1) // tk + 1)]
    qtab = jnp.asarray(np.array([p[0] for p in pairs], np.int32))
    ktab = jnp.asarray(np.array([p[1] for p in pairs], np.int32))
    q_spec = pl.BlockSpec((LANES, tq), lambda b, hh, s, qt_, kt_: (hh, b * nq + qt_[s]))
    k_spec = pl.BlockSpec((tk, LANES), lambda b, hh, s, qt_, kt_: (b * nk + kt_[s], hh))
    v_spec = pl.BlockSpec((LANES, tk), lambda b, hh, s, qt_, kt_: (hh, b * nk + kt_[s]))
    o_spec = pl.BlockSpec((tq, LANES), lambda b, hh, s, qt_, kt_: (b * nq + qt_[s], hh))
    const = lambda shape: pl.BlockSpec(shape, lambda b, hh, s, qt_, kt_: (0,) * len(shape))
    grid_spec = pltpu.PrefetchScalarGridSpec(
        num_scalar_prefetch=2,
        grid=(batch, h, len(pairs)),
        in_specs=[const(lam_vecs.shape), const((LANES, 1)), q_spec, k_spec, v_spec],
        out_specs=o_spec,
        scratch_shapes=[pltpu.VMEM((LANES, tq), jnp.bfloat16), pltpu.VMEM((LANES, tq), jnp.bfloat16),
                        pltpu.VMEM((2, 1, tq), jnp.float32), pltpu.VMEM((2, 1, tq), jnp.float32),
                        pltpu.VMEM((2, LANES, tq), jnp.float32)],
    )
    return pl.pallas_call(
        functools.partial(_flash_kernel, tq=tq, tk=tk, qb=math.gcd(tq, ATTN_QB), lam_init=lam_init),
        grid_spec=grid_spec,
        out_shape=jax.ShapeDtypeStruct((t, n), jnp.bfloat16),
        compiler_params=_cparams("parallel", "parallel", "arbitrary"),
        name="flash_diff_attn",
    )(qtab, ktab, lam_vecs, subln.reshape(LANES, 1), qt, k, vt)


def _decode_kernel(pt_ref, lam_ref, sub_ref, q_ref, kc_ref, vc_ref, *refs, n_pp, head_dim, lam_init):
    f32 = jnp.float32
    bf16 = jnp.bfloat16
    k_refs = refs[:n_pp]
    v_refs = refs[n_pp:2 * n_pp]
    o_ref, qm_ref, m_ref, l_ref, acc_ref = refs[2 * n_pp:]
    j = pl.program_id(1)
    n = q_ref.shape[-1]
    r = n // head_dim
    row = lax.broadcasted_iota(jnp.int32, (r, n), 0)
    lane = lax.broadcasted_iota(jnp.int32, (r, n), 1)
    own_qk = (lane // head_dim) == row
    own_v = (lane // (2 * head_dim)) == (row // 2)

    @pl.when(j == 0)
    def _():
        qm = jnp.where(own_qk, jnp.broadcast_to(q_ref[0].astype(f32), (r, n)), 0.0)
        qm_ref[...] = qm.astype(bf16)
        s_cur = jnp.sum(qm * kc_ref[0].astype(f32), axis=-1, keepdims=True)
        m_ref[...] = s_cur
        l_ref[...] = jnp.ones(l_ref.shape, f32)
        acc_ref[...] = jnp.where(own_v, jnp.broadcast_to(vc_ref[0].astype(f32), (r, n)), 0.0)

    qm = qm_ref[...]
    s = jnp.concatenate(
        [lax.dot_general(qm, kr[0].astype(bf16), _NT, preferred_element_type=f32) for kr in k_refs], axis=1)
    m_old = m_ref[...]
    m_new = jnp.maximum(m_old, jnp.max(s, axis=-1, keepdims=True))
    alpha = jnp.exp2(m_old - m_new)
    p = jnp.exp2(s - m_new)
    l_ref[...] = alpha * l_ref[...] + jnp.sum(p, axis=-1, keepdims=True)
    m_ref[...] = m_new
    page = k_refs[0].shape[1]
    pv = jnp.zeros((r, n), f32)
    for i, vr in enumerate(v_refs):
        pv = pv + jnp.dot(p[:, i * page:(i + 1) * page].astype(bf16), vr[0].astype(bf16),
                          preferred_element_type=f32)
    acc_ref[...] = alpha * acc_ref[...] + pv

    @pl.when(j == pl.num_programs(1) - 1)
    def _():
        lam = _lambda(lam_ref, lam_init)
        rr = lax.broadcasted_iota(jnp.int32, (r, 1), 0)
        coef = jnp.where(rr % 2 == 0, 1.0, -lam) / l_ref[...]
        o = jnp.sum(jnp.where(own_v, acc_ref[...] * coef, 0.0), axis=0, keepdims=True)
        parts = []
        for hh in range(n // LANES):
            oh = o[:, hh * LANES:(hh + 1) * LANES]
            parts.append(oh * _rms_scale(oh) * sub_ref[...] * (1.0 - lam_init))
        o_ref[0] = jnp.concatenate(parts, axis=1).astype(o_ref.dtype)


def _decode_attention(q, k_cur, v_cur, cache_k, cache_v, page_table, lam_vecs, subln, head_dim, lam_init):
    bs, n = q.shape
    n_pages = page_table.shape[1]
    n_pp = math.gcd(DECODE_PAGES_PER_STEP, n_pages)
    page = cache_k.shape[1]
    ck = cache_k.reshape(cache_k.shape[0], page, n)
    cv = cache_v.reshape(cache_v.shape[0], page, n)
    r = n // head_dim
    vec = pl.BlockSpec((1, 1, n), lambda b, j, pt: (b, 0, 0))
    page_spec = lambda i: pl.BlockSpec((1, page, n), lambda b, j, pt: (pt[b, j * n_pp + i], 0, 0))
    grid_spec = pltpu.PrefetchScalarGridSpec(
        num_scalar_prefetch=1,
        grid=(bs, n_pages // n_pp),
        in_specs=[pl.BlockSpec(lam_vecs.shape, lambda b, j, pt: (0, 0)),
                  pl.BlockSpec((1, LANES), lambda b, j, pt: (0, 0)), vec, vec, vec]
                 + [page_spec(i) for i in range(n_pp)] * 2,
        out_specs=vec,
        scratch_shapes=[pltpu.VMEM((r, n), jnp.bfloat16), pltpu.VMEM((r, 1), jnp.float32),
                        pltpu.VMEM((r, 1), jnp.float32), pltpu.VMEM((r, n), jnp.float32)],
    )
    o = pl.pallas_call(
        functools.partial(_decode_kernel, n_pp=n_pp, head_dim=head_dim, lam_init=lam_init),
        grid_spec=grid_spec,
        out_shape=jax.ShapeDtypeStruct((bs, 1, n), jnp.bfloat16),
        compiler_params=_cparams("parallel", "arbitrary"),
        name="paged_decode_attn",
    )(page_table, lam_vecs, subln.reshape(1, LANES), q.reshape(bs, 1, n), k_cur.reshape(bs, 1, n),
      v_cur.reshape(bs, 1, n), *([ck] * n_pp), *([cv] * n_pp))
    return o.reshape(bs, n)


def kernel(x_prompt, x_sample, cache_k, cache_v, state_ssm_re, state_ssm_im, page_table, mix_norm, ffn_norm, kv_norm, final_norm, ssm_a_re, ssm_a_im, ssm_log_dt, ssm_b_re, ssm_b_im, ssm_c_re, ssm_c_im, ssm_d, glu_w_a, glu_w_b, w_q, w_k, w_v, lambda_q1, lambda_k1, lambda_q2, lambda_k2, subln, w_o, ffn_w_gate, ffn_w_up, ffn_w_down):
    bf16 = jnp.bfloat16
    b, l, d = x_prompt.shape
    bs, dec_seq, _ = x_sample.shape
    assert dec_seq == 1 and mix_norm.shape[0] == 2 and ssm_a_re.shape[0] == 1 and w_q.shape[0] == 1
    head_dim = subln.shape[1] // 2
    n_heads = w_q.shape[2] // (2 * head_dim)
    assert 2 * head_dim == LANES
    past_len = page_table.shape[1] * cache_k.shape[1]
    attn_layer = 1
    lam_init = 0.8 - 0.6 * math.exp(-0.3 * attn_layer)
    q_scale = head_dim ** -0.5 * math.log2(math.e)

    xp = x_prompt.reshape(b * l, d)
    xs = x_sample.reshape(bs, d)
    cast = lambda w: w.astype(bf16)
    wa, wb = cast(glu_w_a[0]), cast(glu_w_b[0])
    wq, wk, wv, wo = cast(w_q[0]), cast(w_k), cast(w_v), cast(w_o[0])
    wg, wu, wd = cast(ffn_w_gate), cast(ffn_w_up), cast(ffn_w_down)
    lam_vecs = jnp.concatenate([lambda_q1, lambda_k1, lambda_q2, lambda_k2], axis=0)

    up = _norm_stepmajor(xp, mix_norm[0], S5_CHUNK)
    us = _norm_cast(xs, mix_norm[0])
    y_p, y_s, (re_p, im_p), (re_s, im_s) = _s5_mixer(
        up, us, state_ssm_re[0], state_ssm_im[0], ssm_a_re[0], ssm_a_im[0], ssm_log_dt[0],
        ssm_b_re[0], ssm_b_im[0], ssm_c_re[0], ssm_c_im[0], b, l)
    glu_w = (mix_norm[0], ssm_d[0], wa, wb)
    x2p = _tail_ffn(xp, (y_p,), glu_w, ffn_norm[0], wg[0], wu[0], wd[0], mode="glu", step_major=True)
    x2s = _tail_ffn(xs, (y_s,), glu_w, ffn_norm[0], wg[0], wu[0], wd[0], mode="glu")

    tab_p = _rope_tables(l, 0, head_dim)
    tab_s = _rope_tables(SUBLANES, past_len, head_dim)
    tab_s = [jnp.broadcast_to(t[:1], (bs, LANES)) for t in tab_s]
    qt_p, k_p, kb_p, v_p, vt_p = _qkv(x2p, mix_norm[1], kv_norm, wq, wk, wv, tab_p, head_dim, q_scale,
                                      seqlen=l, col_tables=_rope_tables_cols(l, head_dim))
    q_s, k_s, kb_s, v_s, vb_s = _qkv(x2s, mix_norm[1], kv_norm, wq, wk, wv, tab_s, head_dim, q_scale)

    n = k_p.shape[1]
    o_p = _flash_attention(qt_p, kb_p, vt_p, lam_vecs, subln[0], lam_init, b, l)
    o_s = _decode_attention(q_s, kb_s, vb_s, cache_k, cache_v, page_table, lam_vecs, subln[0],
                            head_dim, lam_init)
    y_prompt = _tail_ffn(x2p, (o_p,), (wo,), ffn_norm[1], wg[1], wu[1], wd[1], mode="attn", g_final=final_norm)
    y_sample = _tail_ffn(x2s, (o_s,), (wo,), ffn_norm[1], wg[1], wu[1], wd[1], mode="attn", g_final=final_norm)

    return (y_prompt.reshape(b, l, d), y_sample.reshape(bs, 1, d),
            k_p.reshape(b, l, n_heads, 2, head_dim), v_p.reshape(b, l, n_heads, 2 * head_dim),
            re_p, im_p,
            k_s.reshape(bs, 1, n_heads, 2, head_dim), v_s.reshape(bs, 1, n_heads, 2 * head_dim),
            re_s, im_s)
```

```python
import functools
import math

import numpy as np

import jax
import jax.numpy as jnp
from jax import lax
from jax.experimental import pallas as pl
from jax.experimental.pallas import tpu as pltpu

EPS = 1e-5
ROPE_THETA = 500000.0
NEG = -1e30
LANES = 128
SUBLANES = 8
VMEM_LIMIT = 56 * 1024 * 1024
S5_CHUNK = 64
ROW_TILE = 512
ATTN_TQ = 1024
ATTN_TK = 512
ATTN_QB = 1024
DECODE_PAGES_PER_STEP = 8

_NT = (((1,), (1,)), ((), ()))


def _cparams(*sem):
    return pltpu.CompilerParams(dimension_semantics=sem, vmem_limit_bytes=VMEM_LIMIT)


def _const_spec(shape):
    nd = len(shape)
    return pl.BlockSpec(shape, lambda *_: (0,) * nd)


def _rms_scale(x):
    return lax.rsqrt(jnp.mean(x * x, axis=-1, keepdims=True) + EPS)


def _row_tile(t):
    return ROW_TILE if t % ROW_TILE == 0 else t


def _norm_cast_kernel(x_ref, g_ref, o_ref):
    x = x_ref[...]
    o_ref[...] = (x * _rms_scale(x) * g_ref[...]).astype(o_ref.dtype)


def _norm_cast(x, g):
    t, d = x.shape
    tm = _row_tile(t)
    return pl.pallas_call(
        _norm_cast_kernel,
        grid=(t // tm,),
        in_specs=[pl.BlockSpec((tm, d), lambda i: (i, 0)), _const_spec((1, d))],
        out_specs=pl.BlockSpec((tm, d), lambda i: (i, 0)),
        out_shape=jax.ShapeDtypeStruct((t, d), jnp.bfloat16),
        compiler_params=_cparams("parallel"),
        name="norm_cast",
    )(x, g.reshape(1, d))


def _norm_stepmajor_kernel(x_ref, g_ref, o_ref, buf_ref, *, chunk):
    x = x_ref[...]
    xn = x * _rms_scale(x) * g_ref[...]
    n_chunks = x.shape[0] // chunk
    for j in range(x.shape[1] // LANES):
        cols = slice(j * LANES, (j + 1) * LANES)
        buf_ref[j] = xn[:, cols]
        for s in range(chunk):
            o_ref[s, :, cols] = buf_ref[j, pl.ds(s, n_chunks, stride=chunk), :].astype(o_ref.dtype)


def _norm_stepmajor(x, g, chunk):
    t, d = x.shape
    chunks_per_tile = 2 * SUBLANES
    tm = chunks_per_tile * chunk
    assert t % tm == 0
    return pl.pallas_call(
        functools.partial(_norm_stepmajor_kernel, chunk=chunk),
        grid=(t // tm,),
        in_specs=[pl.BlockSpec((tm, d), lambda i: (i, 0)), _const_spec((1, d))],
        out_specs=pl.BlockSpec((chunk, chunks_per_tile, d), lambda i: (0, i, 0)),
        out_shape=jax.ShapeDtypeStruct((chunk, t // chunk, d), jnp.bfloat16),
        scratch_shapes=[pltpu.VMEM((d // LANES, tm, LANES), jnp.float32)],
        compiler_params=_cparams("parallel"),
        name="norm_stepmajor",
    )(x, g.reshape(1, d))


def _s5_kernel(a2r_ref, a2i_ref, ldt_ref, bcat_ref, ccat_ref, ug_ref,
               us_ref, h0r_ref, h0i_ref, acol_ref, bnat_ref, cnat_ref,
               y_ref, hfin_ref, ys_ref, hsr_ref, hsi_ref,
               u_ref, tab_ref, w0_ref, w1_ref, wb_ref, toe_ref, *, chunk, n_chunks, batch, hg):
    f32 = jnp.float32
    bf16 = jnp.bfloat16
    C = chunk
    K = C * hg
    gpt = LANES // hg
    half = LANES // 2
    gi = pl.program_id(1)
    lane = lax.broadcasted_iota(jnp.int32, (1, LANES), 1)
    sgn_mp = jnp.where(lane < half, -1.0, 1.0).astype(f32)
    swap = lambda v: pltpu.roll(v, half, axis=1)

    for s8 in range(C // gpt):
        tile = None
        for ss in range(gpt):
            xs = ug_ref[s8 * gpt + ss].astype(f32)
            moved = pltpu.roll(xs, ((ss - gi) * hg) & (LANES - 1), axis=1)
            moved = jnp.where(lane // hg == ss, moved, 0.0)
            tile = moved if tile is None else tile + moved
        u_ref[:, s8 * LANES:(s8 + 1) * LANES] = tile.astype(bf16)

    a_r = a2r_ref[0]
    a_i = a2i_ref[0]
    dt = jnp.exp(ldt_ref[0])
    mag = jnp.exp(dt * a_r)
    ang = dt * a_i
    ab_r = mag * jnp.cos(ang)
    ab_i = mag * jnp.sin(ang)
    den = a_r * a_r + a_i * a_i
    n_r = ab_r - 1.0
    n_i = ab_i
    f_r = (n_r * a_r + n_i * a_i) / den
    f_i = (n_i * a_r - n_r * a_i) / den
    bcat = bcat_ref[0]
    bb1 = f_r * bcat + f_i * (swap(bcat) * sgn_mp)
    bb2 = swap(bb1) * sgn_mp
    ccat = ccat_ref[0]
    cc1 = ccat * (-sgn_mp)
    cc2 = -swap(ccat)

    kk = lax.broadcasted_iota(jnp.int32, (C, 1), 0)

    def powtab(expo):
        t_r = jnp.ones((C, LANES), f32)
        t_i = jnp.zeros((C, LANES), f32)
        p_r, p_i = ab_r, ab_i
        for j in range(C.bit_length() - 1):
            bit = ((expo >> j) & 1) == 1
            m_r = t_r * p_r - t_i * p_i
            m_i = t_r * p_i + t_i * p_r
            t_r = jnp.where(bit, m_r, t_r)
            t_i = jnp.where(bit, m_i, t_i)
            p_r, p_i = p_r * p_r - p_i * p_i, 2.0 * p_r * p_i
        return t_r, t_i, p_r, p_i

    t0_r, t0_i, ac_r, ac_i = powtab(kk)
    tr_r, tr_i, _, _ = powtab(C - 1 - kk)
    tab_ref[0] = t0_r
    tab_ref[1] = t0_i
    tab_ref[2] = t0_r * ab_r - t0_i * ab_i
    tab_ref[3] = t0_r * ab_i + t0_i * ab_r
    tab_ref[4] = tr_r
    tab_ref[5] = tr_i

    for k in range(C):
        rows = pl.ds(k * hg, hg)
        w0_ref[rows, :] = tab_ref[0, k:k + 1, :] * cc1 + tab_ref[1, k:k + 1, :] * cc2
        w1_ref[rows, :] = (tab_ref[2, k:k + 1, :] * cc1 + tab_ref[3, k:k + 1, :] * cc2).astype(bf16)
        wb_ref[rows, :] = (tab_ref[4, k:k + 1, :] * bb1 + tab_ref[5, k:k + 1, :] * bb2).astype(bf16)

    kt = lax.dot_general(bb1, w0_ref[...], _NT, precision=lax.Precision.HIGHEST,
                         preferred_element_type=f32)
    col = lax.broadcasted_iota(jnp.int32, (1, K), 1)
    for s in range(C):
        shifted = kt if s == 0 else jnp.where(col >= s * hg, pltpu.roll(kt, s * hg, axis=1), 0.0)
        toe_ref[pl.ds(s * hg, hg), :] = shifted.astype(bf16)

    u = u_ref[...]
    m_rows = batch * n_chunks
    x = jnp.dot(u, wb_ref[...], preferred_element_type=f32)
    cidx = lax.broadcasted_iota(jnp.int32, (m_rows, 1), 0) % n_chunks
    p_r, p_i = ac_r, ac_i
    sh = 1
    while sh < n_chunks:
        xs = jnp.where(cidx >= sh, pltpu.roll(x, sh, axis=0), 0.0)
        x = x + xs * p_r + swap(xs) * (p_i * sgn_mp)
        p_r, p_i = p_r * p_r - p_i * p_i, 2.0 * p_r * p_i
        sh *= 2
    for b in range(batch):
        hfin_ref[0, b:b + 1, :] = x[(b + 1) * n_chunks - 1:(b + 1) * n_chunks, :]
    hprev = jnp.where(cidx >= 1, pltpu.roll(x, 1, axis=0), 0.0).astype(bf16)
    y = jnp.dot(u, toe_ref[...], preferred_element_type=f32)
    y = y + lax.dot_general(hprev, w1_ref[...], _NT, preferred_element_type=f32)

    own = jnp.broadcast_to((lane // hg) == gi, (m_rows, LANES))
    for t8 in range(C // gpt):
        tile = y[:, t8 * LANES:(t8 + 1) * LANES]
        for tt in range(gpt):
            moved = pltpu.roll(tile, ((gi - tt) * hg) & (LANES - 1), axis=1)
            pltpu.store(y_ref.at[t8 * gpt + tt], moved, mask=own)

    ca_r = acol_ref[0, 0]
    ca_i = acol_ref[0, 1]
    cdt = jnp.exp(acol_ref[0, 2])
    cmag = jnp.exp(cdt * ca_r)
    cang = cdt * ca_i
    cab_r = cmag * jnp.cos(cang)
    cab_i = cmag * jnp.sin(cang)
    cden = ca_r * ca_r + ca_i * ca_i
    cn_r = cab_r - 1.0
    cf_r = (cn_r * ca_r + cab_i * ca_i) / cden
    cf_i = (cab_i * ca_r - cn_r * ca_i) / cden
    b_r = bnat_ref[0, 0]
    b_i = bnat_ref[0, 1]
    ust = us_ref[...]
    bu_r = jnp.dot((cf_r * b_r - cf_i * b_i).astype(bf16), ust, preferred_element_type=f32)
    bu_i = jnp.dot((cf_r * b_i + cf_i * b_r).astype(bf16), ust, preferred_element_type=f32)
    h0_r = h0r_ref[0]
    h0_i = h0i_ref[0]
    hn_r = cab_r * h0_r - cab_i * h0_i + bu_r
    hn_i = cab_r * h0_i + cab_i * h0_r + bu_i
    hsr_ref[0] = hn_r
    hsi_ref[0] = hn_i
    ys_ref[...] = (jnp.dot(cnat_ref[0, 0].astype(bf16), hn_r.astype(bf16), preferred_element_type=f32)
                   - jnp.dot(cnat_ref[0, 1].astype(bf16), hn_i.astype(bf16), preferred_element_type=f32))


def _s5_mixer(u_sm, u_s, h0_re, h0_im, a_re, a_im, log_dt, b_re, b_im, c_re, c_im, batch, seqlen):
    g, p = a_re.shape
    c, m, d = u_sm.shape
    hg = d // g
    gpt = LANES // hg
    assert 2 * p == LANES and c == S5_CHUNK and c & (c - 1) == 0 and c % gpt == 0 and g % gpt == 0
    nc = seqlen // c
    assert m == batch * nc
    k = c * hg
    bs = u_s.shape[0]
    dup = lambda v: jnp.concatenate([v, v], axis=-1).reshape(g, 1, 2 * p)
    a2r, a2i = dup(a_re), dup(a_im)
    ldt = jnp.broadcast_to(log_dt.reshape(g, 1, 1), (g, 1, LANES))
    bcat = jnp.concatenate([jnp.swapaxes(b_re, 1, 2), jnp.swapaxes(b_im, 1, 2)], axis=-1)
    ccat = jnp.concatenate([c_re, c_im], axis=-1)
    ust = u_s.T
    h0r = h0_re.transpose(1, 2, 0)
    h0i = h0_im.transpose(1, 2, 0)
    acol = jnp.stack([a_re, a_im, jnp.broadcast_to(log_dt[:, None], (g, p))], axis=1)[..., None]
    bnat = jnp.stack([b_re, b_im], axis=1)
    cnat = jnp.stack([c_re, c_im], axis=1)

    grp = lambda *shape: pl.BlockSpec((1,) + shape, lambda j, i: (j * gpt + i,) + (0,) * len(shape))
    rows = pl.BlockSpec((hg, bs), lambda j, i: (j * gpt + i, 0))
    tile = pl.BlockSpec((c, m, LANES), lambda j, i: (0, 0, j))
    y, hfin, yst, hsr, hsi = pl.pallas_call(
        functools.partial(_s5_kernel, chunk=c, n_chunks=nc, batch=batch, hg=hg),
        grid=(g // gpt, gpt),
        in_specs=[grp(1, LANES), grp(1, LANES), grp(1, LANES), grp(hg, LANES), grp(hg, LANES),
                  tile, rows, grp(p, bs), grp(p, bs), grp(3, p, 1), grp(2, p, hg), grp(2, hg, p)],
        out_specs=[tile, grp(batch, LANES), rows, grp(p, bs), grp(p, bs)],
        out_shape=[jax.ShapeDtypeStruct((c, m, d), jnp.float32),
                   jax.ShapeDtypeStruct((g, batch, LANES), jnp.float32),
                   jax.ShapeDtypeStruct((d, bs), jnp.float32),
                   jax.ShapeDtypeStruct((g, p, bs), jnp.float32),
                   jax.ShapeDtypeStruct((g, p, bs), jnp.float32)],
        scratch_shapes=[pltpu.VMEM((m, k), jnp.bfloat16),
                        pltpu.VMEM((6, c, LANES), jnp.float32),
                        pltpu.VMEM((k, LANES), jnp.float32),
                        pltpu.VMEM((k, LANES), jnp.bfloat16),
                        pltpu.VMEM((k, LANES), jnp.bfloat16),
                        pltpu.VMEM((k, k), jnp.bfloat16)],
        compiler_params=_cparams("parallel", "arbitrary"),
        name="s5_mixer",
    )(a2r, a2i, ldt, bcat, ccat, u_sm, ust, h0r, h0i, acol, bnat, cnat)
    split = lambda h: (h[..., :p].transpose(1, 0, 2)[None], h[..., p:].transpose(1, 0, 2)[None])
    return y, yst.T, split(hfin), (hsr.transpose(2, 0, 1)[None], hsi.transpose(2, 0, 1)[None])


def _ff_chunk(d_ff):
    for n in (2, 1, 4, 11, 22):
        if d_ff % n == 0 and (d_ff // n) % LANES == 0:
            return d_ff // n
    return d_ff


def _tail_ffn_kernel(*refs, mode, final, ff_chunk, step_major):
    f32 = jnp.float32
    bf16 = jnp.bfloat16
    refs = list(refs)
    if step_major:
        ybuf_ref = refs.pop()
    o_ref = refs.pop()
    x_ref = refs.pop(0)
    x = x_ref[...]
    if mode == "glu":
        ys_ref, gm_ref, d_ref, wa_ref, wb_ref = refs[:5]
        refs = refs[5:]
        if step_major:
            chunk, per_tile = ys_ref.shape[0], ys_ref.shape[1]
            parts = []
            for j in range(ys_ref.shape[2] // LANES):
                for s in range(chunk):
                    ybuf_ref[j, pl.ds(s, per_tile, stride=chunk), :] = ys_ref[s, :, j * LANES:(j + 1) * LANES]
                parts.append(ybuf_ref[j])
            ys = jnp.concatenate(parts, axis=1)
        else:
            ys = ys_ref[...]
        un = x * _rms_scale(x) * gm_ref[...]
        z = jax.nn.gelu(ys + d_ref[...] * un).astype(bf16)
        mix = (jnp.dot(z, wa_ref[...], preferred_element_type=f32)
               * jax.nn.sigmoid(jnp.dot(z, wb_ref[...], preferred_element_type=f32)))
    else:
        o_in_ref, wo_ref = refs[:2]
        refs = refs[2:]
        mix = jnp.dot(o_in_ref[...], wo_ref[...], preferred_element_type=f32)
    gf_ref, wg_ref, wu_ref, wd_ref = refs[:4]
    x1 = x + mix
    hn = (x1 * _rms_scale(x1) * gf_ref[...]).astype(bf16)
    d_ff = wg_ref.shape[1]
    acc = x1
    for c0 in range(0, d_ff, ff_chunk):
        cols = pl.ds(c0, ff_chunk)
        gate = jnp.dot(hn, wg_ref[:, cols], preferred_element_type=f32)
        up = jnp.dot(hn, wu_ref[:, cols], preferred_element_type=f32)
        act = (jax.nn.silu(gate) * up).astype(bf16)
        acc = acc + jnp.dot(act, wd_ref[cols, :], preferred_element_type=f32)
    if final:
        acc = acc * _rms_scale(acc) * refs[4][...]
    o_ref[...] = acc


def _tail_ffn(x, mix_in, mix_w, g_ffn, wg, wu, wd, *, mode, g_final=None, step_major=False):
    t, d = x.shape
    d_ff = wg.shape[1]
    tm = _row_tile(t)
    row = lambda w: pl.BlockSpec((tm, w), lambda i: (i, 0))
    res = lambda shape: pl.BlockSpec(shape, lambda i: (0,) * len(shape), pipeline_mode=pl.Buffered(1))
    scratch = []
    if mode == "glu":
        ys, = mix_in
        g_mix, dvec, wa, wb = mix_w
        if step_major:
            chunk = ys.shape[0]
            assert tm % (chunk * SUBLANES) == 0
            ys_spec = pl.BlockSpec((chunk, tm // chunk, d), lambda i: (0, i, 0))
            scratch = [pltpu.VMEM((d // LANES, tm, LANES), jnp.float32)]
        else:
            ys_spec = row(d)
        args = [x, ys, g_mix.reshape(1, d), dvec.reshape(1, d), wa, wb]
        specs = [row(d), ys_spec, res((1, d)), res((1, d)), res(wa.shape), res(wb.shape)]
    else:
        o_in, = mix_in
        wo, = mix_w
        args = [x, o_in, wo]
        specs = [row(d), row(o_in.shape[1]), res(wo.shape)]
    args += [g_ffn.reshape(1, d), wg, wu, wd]
    specs += [res((1, d)), res(wg.shape), res(wu.shape), res(wd.shape)]
    if g_final is not None:
        args.append(g_final.reshape(1, d))
        specs.append(res((1, d)))
    return pl.pallas_call(
        functools.partial(_tail_ffn_kernel, mode=mode, final=g_final is not None, ff_chunk=_ff_chunk(d_ff),
                          step_major=step_major),
        grid=(t // tm,),
        in_specs=specs,
        out_specs=row(d),
        out_shape=jax.ShapeDtypeStruct((t, d), jnp.float32),
        scratch_shapes=scratch,
        compiler_params=_cparams("parallel"),
        name="tail_ffn_" + mode,
    )(*args)


def _rope_inv(rot_dim):
    return ROPE_THETA ** (-jnp.arange(0, rot_dim, 2, dtype=jnp.float32) / rot_dim)


def _rope_kernel(inv_ref, cos_ref, sa_ref, sb_ref, *, pos0, rot_dim, head_dim):
    tm = cos_ref.shape[0]
    half = rot_dim // 2
    pos = (pos0 + pl.program_id(0) * tm + lax.broadcasted_iota(jnp.int32, (tm, 1), 0)).astype(jnp.float32)
    dd = lax.broadcasted_iota(jnp.int32, (1, LANES), 1) % head_dim
    first = dd < half
    second = (dd >= half) & (dd < rot_dim)
    ang = pos * inv_ref[...]
    c = jnp.cos(ang)
    s = jnp.sin(ang)
    cos_ref[...] = jnp.where(first | second, c, 1.0)
    sa_ref[...] = jnp.where(first, -s, 0.0)
    sb_ref[...] = jnp.where(second, s, 0.0)


def _rope_tables(n_pos, pos0, head_dim):
    rot_dim = head_dim // 4
    half = rot_dim // 2
    dd = jnp.arange(LANES) % head_dim
    inv_lane = jnp.where(dd < rot_dim, _rope_inv(rot_dim)[dd % half], 0.0).reshape(1, LANES)
    tm = _row_tile(n_pos)
    out = jax.ShapeDtypeStruct((n_pos, LANES), jnp.float32)
    spec = pl.BlockSpec((tm, LANES), lambda i: (i, 0))
    return pl.pallas_call(
        functools.partial(_rope_kernel, pos0=pos0, rot_dim=rot_dim, head_dim=head_dim),
        grid=(n_pos // tm,),
        in_specs=[_const_spec((1, LANES))],
        out_specs=[spec, spec, spec],
        out_shape=[out, out, out],
        compiler_params=_cparams("parallel"),
        name="rope_tables",
    )(inv_lane)


def _rope_cols_kernel(inv_ref, cos_ref, sin_ref):
    n = cos_ref.shape[1]
    pos = lax.broadcasted_iota(jnp.int32, (1, n), 1).astype(jnp.float32)
    ang = inv_ref[...] * pos
    cos_ref[...] = jnp.cos(ang)
    sin_ref[...] = jnp.sin(ang)


def _rope_tables_cols(n_pos, head_dim):
    half = head_dim // 8
    out = jax.ShapeDtypeStruct((half, n_pos), jnp.float32)
    return pl.pallas_call(
        _rope_cols_kernel,
        out_shape=[out, out],
        name="rope_tables_cols",
    )(_rope_inv(2 * half).reshape(half, 1))


def _rope_lanes(t, cos, sa, sb, half):
    cols = []
    for j in range(t.shape[1] // LANES):
        ts = t[:, j * LANES:(j + 1) * LANES]
        cols.append(ts * cos + pltpu.roll(ts, LANES - half, axis=1) * sa + pltpu.roll(ts, half, axis=1) * sb)
    return jnp.concatenate(cols, axis=1)


def _rope_rows(t, cosc, sinc, half, head_dim, scale, o_ref, o_idx):
    dt = o_ref.dtype
    for r0 in range(0, t.shape[0], head_dim):
        x1 = t[r0:r0 + half]
        x2 = t[r0 + half:r0 + 2 * half]
        rot = jnp.concatenate([x1 * cosc - x2 * sinc, x2 * cosc + x1 * sinc], axis=0)
        o_ref[o_idx + (slice(r0, r0 + 2 * half), slice(None))] = (rot * scale).astype(dt)
        o_ref[o_idx + (slice(r0 + 2 * half, r0 + head_dim), slice(None))] = (
            t[r0 + 2 * half:r0 + head_dim] * scale).astype(dt)


def _qkv_t_kernel(x_ref, gq_ref, gkv_ref, wqt_ref, wkt_ref, wv_ref, wvt_ref, cosc_ref, sinc_ref,
                  q_ref, k_ref, kb_ref, v_ref, vb_ref, *, half, head_dim, q_scale):
    f32 = jnp.float32
    bf16 = jnp.bfloat16
    x = x_ref[...]
    xr = x * _rms_scale(x)
    hq = (xr * gq_ref[...]).astype(bf16)
    hkv = (xr * gkv_ref[...]).astype(bf16)
    cosc = cosc_ref[...]
    sinc = sinc_ref[...]
    kt = lax.dot_general(wkt_ref[...], hkv, _NT, preferred_element_type=f32)
    _rope_rows(kt, cosc, sinc, half, head_dim, 1.0, k_ref, (0,))
    kb_ref[...] = k_ref[0].T.astype(bf16)
    v_ref[...] = jnp.dot(hkv, wv_ref[...], preferred_element_type=f32)
    vb_ref[...] = lax.dot_general(wvt_ref[...], hkv, _NT, preferred_element_type=f32).astype(bf16)
    qt = lax.dot_general(wqt_ref[...], hq, _NT, preferred_element_type=f32)
    _rope_rows(qt, cosc, sinc, half, head_dim, q_scale, q_ref, ())


def _qkv_t(x, g_q, g_kv, wq, wk, wv, col_tables, head_dim, q_scale, batch, seqlen):
    t, d = x.shape
    n = wk.shape[1]
    tm = _row_tile(seqlen)
    half = head_dim // 8
    assert half == SUBLANES and t == batch * seqlen
    tiles = seqlen // tm
    row = lambda w: pl.BlockSpec((tm, w), lambda i: (i, 0))
    col = pl.BlockSpec((n, tm), lambda i: (0, i))
    ctab = pl.BlockSpec((half, tm), lambda i: (0, i % tiles))
    wt = _const_spec((n, d))
    return pl.pallas_call(
        functools.partial(_qkv_t_kernel, half=half, head_dim=head_dim, q_scale=q_scale),
        grid=(t // tm,),
        in_specs=[row(d), _const_spec((1, d)), _const_spec((1, d)), wt, wt, _const_spec((d, n)), wt, ctab, ctab],
        out_specs=[col, pl.BlockSpec((1, n, tm), lambda i: (i // tiles, 0, i % tiles)), row(n), row(n), col],
        out_shape=[jax.ShapeDtypeStruct((n, t), jnp.bfloat16),
                   jax.ShapeDtypeStruct((batch, n, seqlen), jnp.float32),
                   jax.ShapeDtypeStruct((t, n), jnp.bfloat16),
                   jax.ShapeDtypeStruct((t, n), jnp.float32),
                   jax.ShapeDtypeStruct((n, t), jnp.bfloat16)],
        compiler_params=_cparams("parallel"),
        name="qkv_rope_t",
    )(x, g_q.reshape(1, d), g_kv.reshape(1, d), wq.T, wk.T, wv, wv.T, *col_tables)


def _qkv_kernel(x_ref, gq_ref, gkv_ref, wq_ref, wk_ref, wv_ref, cos_ref, sa_ref, sb_ref,
                q_ref, k_ref, kb_ref, v_ref, vb_ref, *, half, q_scale):
    f32 = jnp.float32
    bf16 = jnp.bfloat16
    x = x_ref[...]
    xr = x * _rms_scale(x)
    hq = (xr * gq_ref[...]).astype(bf16)
    hkv = (xr * gkv_ref[...]).astype(bf16)
    cos = cos_ref[...]
    sa = sa_ref[...]
    sb = sb_ref[...]
    k = _rope_lanes(jnp.dot(hkv, wk_ref[...], preferred_element_type=f32), cos, sa, sb, half)
    k_ref[...] = k
    kb_ref[...] = k.astype(bf16)
    v = jnp.dot(hkv, wv_ref[...], preferred_element_type=f32)
    v_ref[...] = v
    vb_ref[...] = v.astype(bf16)
    q = _rope_lanes(jnp.dot(hq, wq_ref[...], preferred_element_type=f32), cos, sa, sb, half)
    q_ref[...] = (q * q_scale).astype(bf16)


def _qkv(x, g_q, g_kv, wq, wk, wv, tables, head_dim, q_scale):
    t, d = x.shape
    n = wk.shape[1]
    tm = _row_tile(t)
    row = lambda w: pl.BlockSpec((tm, w), lambda i: (i, 0))
    f32o = jax.ShapeDtypeStruct((t, n), jnp.float32)
    bf16o = jax.ShapeDtypeStruct((t, n), jnp.bfloat16)
    return pl.pallas_call(
        functools.partial(_qkv_kernel, half=head_dim // 8, q_scale=q_scale),
        grid=(t // tm,),
        in_specs=[row(d), _const_spec((1, d)), _const_spec((1, d)),
                  _const_spec(wq.shape), _const_spec(wk.shape), _const_spec(wv.shape)] + [row(LANES)] * 3,
        out_specs=[row(n)] * 5,
        out_shape=[bf16o, f32o, bf16o, f32o, bf16o],
        compiler_params=_cparams("parallel"),
        name="qkv_rope",
    )(x, g_q.reshape(1, d), g_kv.reshape(1, d), wq, wk, wv, *tables)


def _lambda(lam_ref, lam_init):
    l = lam_ref[...]
    return (jnp.exp(jnp.sum(l[0:1] * l[1:2], axis=-1, keepdims=True))
            - jnp.exp(jnp.sum(l[2:3] * l[3:4], axis=-1, keepdims=True)) + lam_init)


def _flash_kernel(qtab_ref, ktab_ref, lam_ref, sub_ref, q_ref, k_ref, v_ref, o_ref,
                  qa_ref, qb_ref, m_ref, l_ref, acc_ref, *, tq, tk, qb, lam_init):
    f32 = jnp.float32
    bf16 = jnp.bfloat16
    step = pl.program_id(2)
    qi = qtab_ref[step]
    ki = ktab_ref[step]
    last = (qi * tq + tq - 1) // tk
    half = LANES // 2

    @pl.when(ki == 0)
    def _():
        q = q_ref[...].astype(f32)
        lo = lax.broadcasted_iota(jnp.int32, (LANES, 1), 0) < half
        qa_ref[...] = jnp.where(lo, q, 0.0).astype(bf16)
        qb_ref[...] = jnp.where(lo, 0.0, q).astype(bf16)
        m_ref[...] = jnp.full(m_ref.shape, -jnp.inf, f32)
        l_ref[...] = jnp.zeros(l_ref.shape, f32)
        acc_ref[...] = jnp.zeros(acc_ref.shape, f32)

    def update(masked):
        k = k_ref[...]
        vt = v_ref[...]
        for j in range(tq // qb):
            cols = slice(j * qb, (j + 1) * qb)
            if masked:
                kpos = ki * tk + lax.broadcasted_iota(jnp.int32, (tk, 1), 0)
                qpos = qi * tq + j * qb + lax.broadcasted_iota(jnp.int32, (1, qb), 1)
                keep = kpos <= qpos
            for c, qc_ref in enumerate((qa_ref, qb_ref)):
                s = jnp.dot(k, qc_ref[:, cols], preferred_element_type=f32)
                if masked:
                    s = jnp.where(keep, s, NEG)
                m_old = m_ref[c, :, cols]
                m_new = jnp.maximum(m_old, jnp.max(s, axis=0, keepdims=True))
                alpha = jnp.exp2(m_old - m_new)
                p = jnp.exp2(s - m_new)
                l_ref[c, :, cols] = alpha * l_ref[c, :, cols] + jnp.sum(p, axis=0, keepdims=True)
                acc_ref[c, :, cols] = alpha * acc_ref[c, :, cols] + jnp.dot(
                    vt, p.astype(bf16), preferred_element_type=f32)
                m_ref[c, :, cols] = m_new

    crosses = ki * tk + tk - 1 > qi * tq

    @pl.when(crosses)
    def _():
        update(True)

    @pl.when(jnp.logical_not(crosses))
    def _():
        update(False)

    @pl.when(ki == last)
    def _():
        lam = _lambda(lam_ref, lam_init)
        ot = acc_ref[0] * (1.0 / l_ref[0]) - lam * (acc_ref[1] * (1.0 / l_ref[1]))
        r = lax.rsqrt(jnp.mean(ot * ot, axis=0, keepdims=True) + EPS)
        ot = ot * r * sub_ref[...] * (1.0 - lam_init)
        o_ref[...] = ot.T.astype(o_ref.dtype)


def _flash_attention(qt, k, vt, lam_vecs, subln, lam_init, batch, seqlen):
    n, t = qt.shape
    h = n // LANES
    tq = ATTN_TQ if seqlen % ATTN_TQ == 0 else seqlen
    tk = ATTN_TK if seqlen % ATTN_TK == 0 else seqlen
    nq, nk = seqlen // tq, seqlen // tk
    pairs = [(qi, ki) for qi in range(nq) for ki in range((qi * tq + tq - 1) // tk + 1)]
    qtab = jnp.asarray(np.array([p[0] for p in pairs], np.int32))
    ktab = jnp.asarray(np.array([p[1] for p in pairs], np.int32))
    q_spec = pl.BlockSpec((LANES, tq), lambda b, hh, s, qt_, kt_: (hh, b * nq + qt_[s]))
    k_spec = pl.BlockSpec((tk, LANES), lambda b, hh, s, qt_, kt_: (b * nk + kt_[s], hh))
    v_spec = pl.BlockSpec((LANES, tk), lambda b, hh, s, qt_, kt_: (hh, b * nk + kt_[s]))
    o_spec = pl.BlockSpec((tq, LANES), lambda b, hh, s, qt_, kt_: (b * nq + qt_[s], hh))
    const = lambda shape: pl.BlockSpec(shape, lambda b, hh, s, qt_, kt_: (0,) * len(shape))
    grid_spec = pltpu.PrefetchScalarGridSpec(
        num_scalar_prefetch=2,
        grid=(batch, h, len(pairs)),
        in_specs=[const(lam_vecs.shape), const((LANES, 1)), q_spec, k_spec, v_spec],
        out_specs=o_spec,
        scratch_shapes=[pltpu.VMEM((LANES, tq), jnp.bfloat16), pltpu.VMEM((LANES, tq), jnp.bfloat16),
                        pltpu.VMEM((2, 1, tq), jnp.float32), pltpu.VMEM((2, 1, tq), jnp.float32),
                        pltpu.VMEM((2, LANES, tq), jnp.float32)],
    )
    return pl.pallas_call(
        functools.partial(_flash_kernel, tq=tq, tk=tk, qb=math.gcd(tq, ATTN_QB), lam_init=lam_init),
        grid_spec=grid_spec,
        out_shape=jax.ShapeDtypeStruct((t, n), jnp.bfloat16),
        compiler_params=_cparams("parallel", "parallel", "arbitrary"),
        name="flash_diff_attn",
    )(qtab, ktab, lam_vecs, subln.reshape(LANES, 1), qt, k, vt)


def _decode_kernel(pt_ref, lam_ref, sub_ref, q_ref, kc_ref, vc_ref, *refs, n_pp, head_dim, lam_init):
    f32 = jnp.float32
    bf16 = jnp.bfloat16
    k_refs = refs[:n_pp]
    v_refs = refs[n_pp:2 * n_pp]
    o_ref, qm_ref, m_ref, l_ref, acc_ref = refs[2 * n_pp:]
    j = pl.program_id(1)
    n = q_ref.shape[-1]
    r = n // head_dim
    row = lax.broadcasted_iota(jnp.int32, (r, n), 0)
    lane = lax.broadcasted_iota(jnp.int32, (r, n), 1)
    own_qk = (lane // head_dim) == row
    own_v = (lane // (2 * head_dim)) == (row // 2)

    @pl.when(j == 0)
    def _():
        qm = jnp.where(own_qk, jnp.broadcast_to(q_ref[0].astype(f32), (r, n)), 0.0)
        qm_ref[...] = qm.astype(bf16)
        s_cur = jnp.sum(qm * kc_ref[0].astype(f32), axis=-1, keepdims=True)
        m_ref[...] = s_cur
        l_ref[...] = jnp.ones(l_ref.shape, f32)
        acc_ref[...] = jnp.where(own_v, jnp.broadcast_to(vc_ref[0].astype(f32), (r, n)), 0.0)

    qm = qm_ref[...]
    s = jnp.concatenate(
        [jnp.dot(qm, kr[0].astype(bf16), preferred_element_type=f32) for kr in k_refs], axis=1)
    m_old = m_ref[...]
    m_new = jnp.maximum(m_old, jnp.max(s, axis=-1, keepdims=True))
    alpha = jnp.exp2(m_old - m_new)
    p = jnp.exp2(s - m_new)
    l_ref[...] = alpha * l_ref[...] + jnp.sum(p, axis=-1, keepdims=True)
    m_ref[...] = m_new
    n_heads = n // LANES
    page = v_refs[0].shape[1] // n_heads
    pv = jnp.zeros((r, n), f32)
    for i, vr in enumerate(v_refs):
        vp = jnp.concatenate([vr[0, pl.ds(hh, page, stride=n_heads), :] for hh in range(n_heads)], axis=1)
        pv = pv + jnp.dot(p[:, i * page:(i + 1) * page].astype(bf16), vp.astype(bf16),
                          preferred_element_type=f32)
    acc_ref[...] = alpha * acc_ref[...] + pv

    @pl.when(j == pl.num_programs(1) - 1)
    def _():
        lam = _lambda(lam_ref, lam_init)
        rr = lax.broadcasted_iota(jnp.int32, (r, 1), 0)
        coef = jnp.where(rr % 2 == 0, 1.0, -lam) / l_ref[...]
        o = jnp.sum(jnp.where(own_v, acc_ref[...] * coef, 0.0), axis=0, keepdims=True)
        parts = []
        for hh in range(n // LANES):
            oh = o[:, hh * LANES:(hh + 1) * LANES]
            parts.append(oh * _rms_scale(oh) * sub_ref[...] * (1.0 - lam_init))
        o_ref[0] = jnp.concatenate(parts, axis=1).astype(o_ref.dtype)


def _decode_attention(q, k_cur, v_cur, cache_k, cache_v, page_table, lam_vecs, subln, head_dim, lam_init):
    bs, n = q.shape
    n_pages = page_table.shape[1]
    n_pp = math.gcd(DECODE_PAGES_PER_STEP, n_pages)
    page = cache_k.shape[1]
    ck = cache_k.transpose(0, 2, 3, 4, 1).reshape(cache_k.shape[0], n, page)
    cv = cache_v.reshape(cache_v.shape[0], page * cache_v.shape[2], cache_v.shape[3])
    assert cv.shape[2] == LANES
    r = n // head_dim
    vec = pl.BlockSpec((1, 1, n), lambda b, j, pt: (b, 0, 0))
    k_spec = lambda i: pl.BlockSpec((1, n, page), lambda b, j, pt: (pt[b, j * n_pp + i], 0, 0))
    v_spec = lambda i: pl.BlockSpec((1,) + cv.shape[1:], lambda b, j, pt: (pt[b, j * n_pp + i], 0, 0))
    grid_spec = pltpu.PrefetchScalarGridSpec(
        num_scalar_prefetch=1,
        grid=(bs, n_pages // n_pp),
        in_specs=[pl.BlockSpec(lam_vecs.shape, lambda b, j, pt: (0, 0)),
                  pl.BlockSpec((1, LANES), lambda b, j, pt: (0, 0)), vec, vec, vec]
                 + [k_spec(i) for i in range(n_pp)] + [v_spec(i) for i in range(n_pp)],
        out_specs=vec,
        scratch_shapes=[pltpu.VMEM((r, n), jnp.bfloat16), pltpu.VMEM((r, 1), jnp.float32),
                        pltpu.VMEM((r, 1), jnp.float32), pltpu.VMEM((r, n), jnp.float32)],
    )
    o = pl.pallas_call(
        functools.partial(_decode_kernel, n_pp=n_pp, head_dim=head_dim, lam_init=lam_init),
        grid_spec=grid_spec,
        out_shape=jax.ShapeDtypeStruct((bs, 1, n), jnp.bfloat16),
        compiler_params=_cparams("parallel", "arbitrary"),
        name="paged_decode_attn",
    )(page_table, lam_vecs, subln.reshape(1, LANES), q.reshape(bs, 1, n), k_cur.reshape(bs, 1, n),
      v_cur.reshape(bs, 1, n), *([ck] * n_pp), *([cv] * n_pp))
    return o.reshape(bs, n)


def kernel(x_prompt, x_sample, cache_k, cache_v, state_ssm_re, state_ssm_im, page_table, mix_norm, ffn_norm, kv_norm, final_norm, ssm_a_re, ssm_a_im, ssm_log_dt, ssm_b_re, ssm_b_im, ssm_c_re, ssm_c_im, ssm_d, glu_w_a, glu_w_b, w_q, w_k, w_v, lambda_q1, lambda_k1, lambda_q2, lambda_k2, subln, w_o, ffn_w_gate, ffn_w_up, ffn_w_down):
    bf16 = jnp.bfloat16
    b, l, d = x_prompt.shape
    bs, dec_seq, _ = x_sample.shape
    assert dec_seq == 1 and mix_norm.shape[0] == 2 and ssm_a_re.shape[0] == 1 and w_q.shape[0] == 1
    head_dim = subln.shape[1] // 2
    n_heads = w_q.shape[2] // (2 * head_dim)
    assert 2 * head_dim == LANES
    past_len = page_table.shape[1] * cache_k.shape[1]
    attn_layer = 1
    lam_init = 0.8 - 0.6 * math.exp(-0.3 * attn_layer)
    q_scale = head_dim ** -0.5 * math.log2(math.e)

    xp = x_prompt.reshape(b * l, d)
    xs = x_sample.reshape(bs, d)
    cast = lambda w: w.astype(bf16)
    wa, wb = cast(glu_w_a[0]), cast(glu_w_b[0])
    wq, wk, wv, wo = cast(w_q[0]), cast(w_k), cast(w_v), cast(w_o[0])
    wg, wu, wd = cast(ffn_w_gate), cast(ffn_w_up), cast(ffn_w_down)
    lam_vecs = jnp.concatenate([lambda_q1, lambda_k1, lambda_q2, lambda_k2], axis=0)

    up = _norm_stepmajor(xp, mix_norm[0], S5_CHUNK)
    us = _norm_cast(xs, mix_norm[0])
    y_p, y_s, (re_p, im_p), (re_s, im_s) = _s5_mixer(
        up, us, state_ssm_re[0], state_ssm_im[0], ssm_a_re[0], ssm_a_im[0], ssm_log_dt[0],
        ssm_b_re[0], ssm_b_im[0], ssm_c_re[0], ssm_c_im[0], b, l)
    glu_w = (mix_norm[0], ssm_d[0], wa, wb)
    x2p = _tail_ffn(xp, (y_p,), glu_w, ffn_norm[0], wg[0], wu[0], wd[0], mode="glu", step_major=True)
    x2s = _tail_ffn(xs, (y_s,), glu_w, ffn_norm[0], wg[0], wu[0], wd[0], mode="glu")

    tab_s = _rope_tables(SUBLANES, past_len, head_dim)
    tab_s = [jnp.broadcast_to(t[:1], (bs, LANES)) for t in tab_s]
    qt_p, kt_p, kb_p, v_p, vt_p = _qkv_t(x2p, mix_norm[1], kv_norm, wq, wk, wv, _rope_tables_cols(l, head_dim),
                                         head_dim, q_scale, b, l)
    q_s, k_s, kb_s, v_s, vb_s = _qkv(x2s, mix_norm[1], kv_norm, wq, wk, wv, tab_s, head_dim, q_scale)

    n = kb_p.shape[1]
    o_p = _flash_attention(qt_p, kb_p, vt_p, lam_vecs, subln[0], lam_init, b, l)
    o_s = _decode_attention(q_s, kb_s, vb_s, cache_k, cache_v, page_table, lam_vecs, subln[0],
                            head_dim, lam_init)
    y_prompt = _tail_ffn(x2p, (o_p,), (wo,), ffn_norm[1], wg[1], wu[1], wd[1], mode="attn", g_final=final_norm)
    y_sample = _tail_ffn(x2s, (o_s,), (wo,), ffn_norm[1], wg[1], wu[1], wd[1], mode="attn", g_final=final_norm)

    return (y_prompt.reshape(b, l, d), y_sample.reshape(bs, 1, d),
            kt_p.reshape(b, n_heads, 2, head_dim, l).transpose(0, 4, 1, 2, 3),
            v_p.reshape(b, l, n_heads, 2 * head_dim),
            re_p, im_p,
            k_s.reshape(bs, 1, n_heads, 2, head_dim), v_s.reshape(bs, 1, n_heads, 2 * head_dim),
            re_s, im_s)
```
